```python
import math
import jax, jax.numpy as jnp
from jax import lax
import numpy as np


D_MODEL = 1024
BATCH = 2
SEQ = 16384
DEPTH = 2

BLOCK = 128
EPS = 1e-6
SB_HEADS = 4
SB_DIM = 64
SB_SUB = 32
MLA_HEADS = 4
MLA_Q_RANK = 256
MLA_KV_RANK = 256
MLA_NOPE = 64
MLA_ROPE = 32
MLA_V = 64
ROPE_THETA = 10000.0
SW_Q_HEADS = 8
SW_KV_HEADS = 2
SW_DIM = 32
SW_WINDOW = 128
DIL_PATTERNS = ((128, 1), (512, 4), (2048, 16))
DIL_HEADS = 4
DIL_DIM = 32
REL_BUCKETS = 32
REL_MAX_DIST = 2048
GATE_RANK = 128
N_BRANCH = 4
D_FF_DENSE = 2048
N_EXPERTS = 8
TOP_K = 2
D_FF_EXPERT = 768
MOE_BLOCK = 512

SB_W = SB_HEADS * SB_DIM
MLA_QK = MLA_NOPE + MLA_ROPE
SW_W = SW_Q_HEADS * SW_DIM
SW_KV_W = SW_KV_HEADS * SW_DIM
N_DIL = len(DIL_PATTERNS)
DIL_W = N_DIL * DIL_HEADS * DIL_DIM
REL_HEADS = SW_Q_HEADS + N_DIL * DIL_HEADS
IN_SIZES = (SB_W, SB_W, SB_W, MLA_Q_RANK, MLA_KV_RANK, MLA_ROPE, SW_W, SW_KV_W, SW_KV_W, DIL_W, DIL_W, DIL_W)
IN_COLS = sum(IN_SIZES)
BR_SIZES = (SB_W, MLA_HEADS * MLA_V, SW_W, DIL_HEADS * DIL_DIM)
BR_ROWS = sum(BR_SIZES)

kernel_name = 'hybrid_parallel_gated_mixers_moe'


def rmsnorm(x, g):
    xf = x.astype(jnp.float32)
    y = xf * lax.rsqrt(jnp.mean(xf * xf, axis=-1, keepdims=True) + EPS)
    return (y * g.astype(jnp.float32)).astype(x.dtype)


def rope(x, pos):
    half = x.shape[-1] // 2
    inv = ROPE_THETA ** (-jnp.arange(half, dtype=jnp.float32) / half)
    ang = pos[:, None] * inv[None, :]
    cos = jnp.cos(ang)[:, None, :]
    sin = jnp.sin(ang)[:, None, :]
    xf = x.astype(jnp.float32)
    x1, x2 = xf[..., :half], xf[..., half:]
    return jnp.concatenate([x1 * cos - x2 * sin, x2 * cos + x1 * sin], axis=-1).astype(x.dtype)


def t5_bucket(dist):
    max_exact = REL_BUCKETS // 2
    d = jnp.maximum(dist, 1).astype(jnp.float32)
    large = max_exact + (jnp.log(d / max_exact) / math.log(REL_MAX_DIST / max_exact) * (REL_BUCKETS - max_exact)).astype(jnp.int32)
    large = jnp.minimum(large, REL_BUCKETS - 1)
    return jnp.where(dist < max_exact, dist, large)


def stick_breaking_attention(q, k, v):
    B_, S, H, Dh = q.shape
    scale = Dh ** -0.5
    qf = q.astype(jnp.float32)
    kf = k.astype(jnp.float32)
    incl = (jnp.arange(SB_SUB)[:, None] >= jnp.arange(SB_SUB)[None, :]).astype(jnp.float32)
    outs = []
    for i in range(S // BLOCK):
        L = (i + 1) * BLOCK
        n_sub = L // SB_SUB
        z = jnp.einsum('bqhd,bkhd->bhqk', qf[:, i * BLOCK:L], kf[:, :L]) * scale
        mask = jnp.arange(L)[None, :] < (i * BLOCK + jnp.arange(BLOCK))[:, None]
        lk = jnp.where(mask, jax.nn.log_sigmoid(-z), 0.0)
        within = jnp.einsum('bhqnj,js->bhqns', lk.reshape(B_, H, BLOCK, n_sub, SB_SUB), incl)
        tot = within[..., 0]
        after = lax.cumsum(tot, axis=3, reverse=True) - tot
        later = (within + after[..., None]).reshape(B_, H, BLOCK, L) - lk
        a = jnp.where(mask, jnp.exp(jax.nn.log_sigmoid(z) + later), 0.0)
        outs.append(jnp.einsum('bhqk,bkhd->bqhd', a.astype(v.dtype), v[:, :L]))
    return jnp.concatenate(outs, axis=1).reshape(B_, S, H * Dh)


def causal_softmax_attention(q, k, v):
    B_, S, H, Dqk = q.shape
    scale = Dqk ** -0.5
    qf = q.astype(jnp.float32)
    kf = k.astype(jnp.float32)
    outs = []
    for i in range(S // BLOCK):
        L = (i + 1) * BLOCK
        s = jnp.einsum('bqhd,bkhd->bhqk', qf[:, i * BLOCK:L], kf[:, :L]) * scale
        mask = jnp.arange(L)[None, :] <= (i * BLOCK + jnp.arange(BLOCK))[:, None]
        p = jax.nn.softmax(jnp.where(mask, s, -jnp.inf), axis=-1)
        outs.append(jnp.einsum('bhqk,bkhd->bqhd', p.astype(v.dtype), v[:, :L]))
    out = jnp.concatenate(outs, axis=1)
    return out.reshape(B_, S, H * v.shape[-1])


def sliding_window_sink_attention(q, k, v, sinks, bias):
    B_, S, Hq, Dh = q.shape
    Hkv = k.shape[2]
    G = Hq // Hkv
    nb = S // BLOCK
    qb = q.reshape(B_, nb, BLOCK, Hkv, G, Dh).astype(jnp.float32)

    def band(t):
        tb = t.reshape(B_, nb, BLOCK, Hkv, Dh)
        prev = jnp.concatenate([jnp.zeros_like(tb[:, :1]), tb[:, :-1]], axis=1)
        return jnp.concatenate([prev, tb], axis=2)

    kb, vb = band(k), band(v)
    s = jnp.einsum('bnqkgd,bnskd->bnkgqs', qb, kb.astype(jnp.float32)) * (Dh ** -0.5)
    s = s + bias.reshape(Hkv, G, BLOCK, 2 * BLOCK).astype(jnp.float32)
    dist = (BLOCK + jnp.arange(BLOCK))[:, None] - jnp.arange(2 * BLOCK)[None, :]
    in_window = (dist >= 0) & (dist < SW_WINDOW)
    kvalid = (jnp.arange(nb)[:, None] * BLOCK + jnp.arange(2 * BLOCK)[None, :] - BLOCK) >= 0
    mask = in_window[None] & kvalid[:, None, :]
    s = jnp.where(mask[None, :, None, None], s, -jnp.inf)
    sink = sinks.astype(jnp.float32).reshape(Hkv, G)[None, None, :, :, None, None]
    m = jnp.maximum(jnp.max(s, axis=-1, keepdims=True), sink)
    p = jnp.exp(s - m)
    denom = jnp.sum(p, axis=-1, keepdims=True) + jnp.exp(sink - m)
    o = jnp.einsum('bnkgqs,bnskd->bnqkgd', (p / denom).astype(v.dtype), vb)
    return o.reshape(B_, S, Hq * Dh)


def dilated_group_attention(q, k, v, bias, window, dil):
    B_, S, H, Dh = q.shape
    M = S // dil
    nb = -(-M // BLOCK)
    Mp = nb * BLOCK
    jmax = window // dil

    def streams(t):
        t = t.astype(jnp.float32).reshape(B_, M, dil, H, Dh).transpose(0, 2, 1, 3, 4)
        t = jnp.pad(t, ((0, 0), (0, 0), (0, Mp - M), (0, 0), (0, 0)))
        return t.reshape(B_, dil, nb, BLOCK, H, Dh)

    def band(t):
        prev = jnp.concatenate([jnp.zeros_like(t[:, :, :1]), t[:, :, :-1]], axis=2)
        return jnp.concatenate([prev, t], axis=3)

    qs = streams(q)
    ks = band(streams(k))
    vs = band(streams(v))
    s = jnp.einsum('brnqhd,brnkhd->brnhqk', qs, ks) * (Dh ** -0.5) + bias.astype(jnp.float32)
    dist = (BLOCK + jnp.arange(BLOCK))[:, None] - jnp.arange(2 * BLOCK)[None, :]
    in_window = (dist >= 0) & (dist <= jmax)
    kvalid = (jnp.arange(nb)[:, None] * BLOCK + jnp.arange(2 * BLOCK)[None, :] - BLOCK) >= 0
    mask = in_window[None] & kvalid[:, None, :]
    s = jnp.where(mask[None, None, :, None], s, -jnp.inf)
    m = jnp.max(s, axis=-1, keepdims=True)
    p = jnp.exp(s - m)
    l = jnp.sum(p, axis=-1, keepdims=True)
    o = jnp.einsum('brnhqk,brnkhd->brnqhd', p / l, vs)
    lse = (m + jnp.log(l))[..., 0].transpose(0, 1, 2, 4, 3)
    o = o.reshape(B_, dil, Mp, H, Dh)[:, :, :M].transpose(0, 2, 1, 3, 4).reshape(B_, S, H, Dh)
    lse = lse.reshape(B_, dil, Mp, H)[:, :, :M].transpose(0, 2, 1, 3).reshape(B_, S, H)
    return o, lse


def dilated_mixture_attention(q, k, v, biases):
    B_, S, _, H, Dh = q.shape
    os_, lses = [], []
    for g, (w, dil) in enumerate(DIL_PATTERNS):
        o, lse = dilated_group_attention(q[:, :, g], k[:, :, g], v[:, :, g], biases[g], w, dil)
        os_.append(o)
        lses.append(lse)
    wgt = jax.nn.softmax(jnp.stack(lses), axis=0)
    out = jnp.sum(wgt[..., None] * jnp.stack(os_), axis=0)
    return out.reshape(B_, S, H * Dh).astype(v.dtype)


def mixer_sublayer(x, norm_g, w_in, g_qa, w_qb, g_kva, w_kvb, qk_g_mla, qk_g_sw, qk_g_dil, sinks, sw_bias, dil_biases, w_gate_a, w_gate_b, b_gate, w_branch, w_out):
    B_, S, D = x.shape
    h = rmsnorm(x, norm_g)
    proj = h @ w_in
    offs = np.cumsum(IN_SIZES)[:-1].tolist()
    (a_q, a_k, a_v, b_cq, b_ckv, b_kpe, c_q, c_k, c_v, d_q, d_k, d_v) = jnp.split(proj, offs, axis=-1)
    pos = jnp.arange(S, dtype=jnp.float32)

    sh = (B_, S, SB_HEADS, SB_DIM)
    out_a = stick_breaking_attention(a_q.reshape(sh), a_k.reshape(sh), a_v.reshape(sh))

    qm = (rmsnorm(b_cq, g_qa) @ w_qb).reshape(B_, S, MLA_HEADS, MLA_QK)
    q_mla = jnp.concatenate([qm[..., :MLA_NOPE], rope(qm[..., MLA_NOPE:], pos)], axis=-1)
    kv = (rmsnorm(b_ckv, g_kva) @ w_kvb).reshape(B_, S, MLA_HEADS, MLA_NOPE + MLA_V)
    k_pe = jnp.broadcast_to(rope(b_kpe[:, :, None, :], pos), (B_, S, MLA_HEADS, MLA_ROPE))
    k_mla = jnp.concatenate([kv[..., :MLA_NOPE], k_pe], axis=-1)
    q_mla = rmsnorm(q_mla, qk_g_mla[0])
    k_mla = rmsnorm(k_mla, qk_g_mla[1])
    out_b = causal_softmax_attention(q_mla, k_mla, kv[..., MLA_NOPE:])

    q_sw = rmsnorm(c_q.reshape(B_, S, SW_Q_HEADS, SW_DIM), qk_g_sw[0])
    k_sw = rmsnorm(c_k.reshape(B_, S, SW_KV_HEADS, SW_DIM), qk_g_sw[1])
    out_c = sliding_window_sink_attention(q_sw, k_sw, c_v.reshape(B_, S, SW_KV_HEADS, SW_DIM), sinks, sw_bias)

    dsh = (B_, S, N_DIL, DIL_HEADS, DIL_DIM)
    q_d = rmsnorm(d_q.reshape(dsh), qk_g_dil[0])
    k_d = rmsnorm(d_k.reshape(dsh), qk_g_dil[1])
    out_d = dilated_mixture_attention(q_d, k_d, d_v.reshape(dsh), dil_biases)

    g_low = h @ w_gate_a
    row_offs = np.concatenate([[0], np.cumsum(BR_SIZES)]).tolist()
    y = None
    for i, o in enumerate((out_a, out_b, out_c, out_d)):
        pre = g_low @ w_gate_b[:, i * D:(i + 1) * D] + b_gate[i * D:(i + 1) * D]
        gate = jax.nn.sigmoid(pre.astype(jnp.float32)).astype(x.dtype)
        term = gate * (o @ w_branch[row_offs[i]:row_offs[i + 1]])
        y = term if y is None else y + term
    return x + y @ w_out


def swiglu(h, w_gu, w_down):
    gu = h @ w_gu
    g, u = jnp.split(gu, 2, axis=-1)
    return (jax.nn.silu(g) * u) @ w_down


def moe_swiglu(h, w_router, b_router, w_gu, w_down):
    D = h.shape[-1]
    tok = h.reshape(-1, D)
    N = tok.shape[0]
    logits = (tok @ w_router).astype(jnp.float32) + b_router.astype(jnp.float32)
    top_val, top_idx = lax.top_k(logits, TOP_K)
    top_w = jax.nn.softmax(top_val, axis=-1)
    NK = N * TOP_K
    slot_e = top_idx.reshape(NK)
    slot_tok = jnp.arange(NK, dtype=jnp.int32) // TOP_K
    slot_w = top_w.reshape(NK)
    order = jnp.argsort(slot_e)
    sorted_e = slot_e[order]
    counts = jnp.zeros((N_EXPERTS,), jnp.int32).at[slot_e].add(1)
    padded = (counts + MOE_BLOCK - 1) // MOE_BLOCK * MOE_BLOCK
    pad_end = jnp.cumsum(padded)
    pad_start = pad_end - padded
    start = jnp.cumsum(counts) - counts
    dest = pad_start[sorted_e] + jnp.arange(NK, dtype=jnp.int32) - start[sorted_e]
    n_blk = -(-NK // MOE_BLOCK) + N_EXPERTS
    P = n_blk * MOE_BLOCK
    row_tok = jnp.zeros((P,), jnp.int32).at[dest].set(slot_tok[order])
    row_w = jnp.zeros((P,), jnp.float32).at[dest].set(slot_w[order])
    blk_start = jnp.arange(n_blk, dtype=jnp.int32) * MOE_BLOCK
    blk_e = jnp.minimum(jnp.sum((pad_end[None, :] <= blk_start[:, None]).astype(jnp.int32), axis=1), N_EXPERTS - 1)
    xs = tok[row_tok].reshape(n_blk, MOE_BLOCK, D)

    def expert_block(args):
        xb, e = args
        return swiglu(xb, w_gu[e], w_down[e])

    yb = lax.map(expert_block, (xs, blk_e)).reshape(P, D)
    y = jax.ops.segment_sum(yb * row_w[:, None].astype(yb.dtype), row_tok, num_segments=N)
    return y.reshape(h.shape)


def setup_inputs(seed: int = 0) -> dict:
    key = jax.random.key(seed)
    ks = jax.random.split(key, 26)
    f32 = jnp.float32

    def nrm(k, shape, scale):
        return jax.random.normal(k, shape, f32) * scale

    def gain(k, shape):
        return 1.0 + 0.05 * jax.random.normal(k, shape, f32)

    n_dense = (DEPTH + 1) // 2
    n_moe = DEPTH // 2
    row_scale = jnp.asarray(np.concatenate([np.full((s,), s ** -0.5, np.float32) for s in BR_SIZES]))
    return {
        'x': nrm(ks[0], (BATCH, SEQ, D_MODEL), 1.0),
        'norm1_g': gain(ks[1], (DEPTH, D_MODEL)),
        'w_in': nrm(ks[2], (DEPTH, D_MODEL, IN_COLS), D_MODEL ** -0.5),
        'g_qa': gain(ks[3], (DEPTH, MLA_Q_RANK)),
        'w_qb': nrm(ks[4], (DEPTH, MLA_Q_RANK, MLA_HEADS * MLA_QK), MLA_Q_RANK ** -0.5),
        'g_kva': gain(ks[5], (DEPTH, MLA_KV_RANK)),
        'w_kvb': nrm(ks[6], (DEPTH, MLA_KV_RANK, MLA_HEADS * (MLA_NOPE + MLA_V)), MLA_KV_RANK ** -0.5),
        'qk_g_mla': gain(ks[7], (DEPTH, 2, MLA_QK)),
        'qk_g_sw': gain(ks[8], (DEPTH, 2, SW_DIM)),
        'qk_g_dil': gain(ks[9], (DEPTH, 2, DIL_DIM)),
        'sinks': nrm(ks[10], (DEPTH, SW_Q_HEADS), 0.5),
        'rel_bias': nrm(ks[11], (REL_BUCKETS, REL_HEADS), 0.5),
        'w_gate_a': nrm(ks[12], (DEPTH, D_MODEL, GATE_RANK), D_MODEL ** -0.5),
        'w_gate_b': nrm(ks[13], (DEPTH, GATE_RANK, N_BRANCH * D_MODEL), GATE_RANK ** -0.5),
        'b_gate': nrm(ks[14], (DEPTH, N_BRANCH * D_MODEL), 0.02),
        'w_branch': nrm(ks[15], (DEPTH, BR_ROWS, D_MODEL), 1.0) * row_scale[None, :, None],
        'w_out': nrm(ks[16], (DEPTH, D_MODEL, D_MODEL), D_MODEL ** -0.5),
        'norm2_g': gain(ks[17], (DEPTH, D_MODEL)),
        'w_gu_dense': nrm(ks[18], (n_dense, D_MODEL, 2 * D_FF_DENSE), D_MODEL ** -0.5),
        'w_down_dense': nrm(ks[19], (n_dense, D_FF_DENSE, D_MODEL), D_FF_DENSE ** -0.5),
        'w_router': nrm(ks[20], (n_moe, D_MODEL, N_EXPERTS), D_MODEL ** -0.5),
        'b_router': nrm(ks[21], (n_moe, N_EXPERTS), 0.01),
        'w_gu_exp': nrm(ks[22], (n_moe, N_EXPERTS, D_MODEL, 2 * D_FF_EXPERT), D_MODEL ** -0.5),
        'w_down_exp': nrm(ks[23], (n_moe, N_EXPERTS, D_FF_EXPERT, D_MODEL), D_FF_EXPERT ** -0.5),
    }


def reference(x, norm1_g, w_in, g_qa, w_qb, g_kva, w_kvb, qk_g_mla, qk_g_sw, qk_g_dil, sinks, rel_bias, w_gate_a, w_gate_b, b_gate, w_branch, w_out, norm2_g, w_gu_dense, w_down_dense, w_router, b_router, w_gu_exp, w_down_exp):
    dist_band = (BLOCK + jnp.arange(BLOCK))[:, None] - jnp.arange(2 * BLOCK)[None, :]
    dist_pos = jnp.maximum(dist_band, 0)
    sw_bias = rel_bias[t5_bucket(dist_pos)][..., :SW_Q_HEADS].transpose(2, 0, 1)
    dil_biases = []
    for g, (w, dil) in enumerate(DIL_PATTERNS):
        lo = SW_Q_HEADS + g * DIL_HEADS
        b = rel_bias[t5_bucket(dist_pos * dil)][..., lo:lo + DIL_HEADS]
        dil_biases.append(b.transpose(2, 0, 1))

    for i in range(DEPTH):
        x = mixer_sublayer(x, norm1_g[i], w_in[i], g_qa[i], w_qb[i], g_kva[i], w_kvb[i], qk_g_mla[i], qk_g_sw[i], qk_g_dil[i], sinks[i], sw_bias, dil_biases, w_gate_a[i], w_gate_b[i], b_gate[i], w_branch[i], w_out[i])
        h = rmsnorm(x, norm2_g[i])
        if i % 2 == 0:
            x = x + swiglu(h, w_gu_dense[i // 2], w_down_dense[i // 2])
        else:
            x = x + moe_swiglu(h, w_router[i // 2], b_router[i // 2], w_gu_exp[i // 2], w_down_exp[i // 2])
    return x
```

```python
import functools
import math

import jax
import jax.numpy as jnp
import numpy as np
from jax import lax
from jax.experimental import pallas as pl
from jax.experimental.pallas import tpu as pltpu

F32 = jnp.float32
BF16 = jnp.bfloat16

D_MODEL = 1024
BLOCK = 128
EPS = 1e-6
SB_HEADS, SB_DIM = 4, 64
MLA_HEADS, MLA_Q_RANK, MLA_KV_RANK = 4, 256, 256
MLA_NOPE, MLA_ROPE, MLA_V = 64, 32, 64
MLA_QK = MLA_NOPE + MLA_ROPE
ROPE_THETA = 10000.0
SW_Q_HEADS, SW_KV_HEADS, SW_DIM = 8, 2, 32
DIL_PATTERNS = ((128, 1), (512, 4), (2048, 16))
DIL_HEADS, DIL_DIM = 4, 32
N_DIL = len(DIL_PATTERNS)
REL_BUCKETS, REL_MAX_DIST = 32, 2048
GATE_RANK, N_BRANCH = 128, 4
D_FF_DENSE = 2048
N_EXPERTS, TOP_K, D_FF_EXPERT = 8, 2, 768

SB_W = SB_HEADS * SB_DIM
SW_W = SW_Q_HEADS * SW_DIM
SW_KV_W = SW_KV_HEADS * SW_DIM
DIL_GW = DIL_HEADS * DIL_DIM
DIL_W = N_DIL * DIL_GW
IN_SIZES = (SB_W, SB_W, SB_W, MLA_Q_RANK, MLA_KV_RANK, MLA_ROPE, SW_W, SW_KV_W, SW_KV_W, DIL_W, DIL_W, DIL_W)
BR_SIZES = (SB_W, MLA_HEADS * MLA_V, SW_W, DIL_GW)

LANES = 128
MLA_PAD = LANES
NEG = -1e30
VMEM_LIMIT = 56 * 1024 * 1024

ROW_TILE = 512
ATT_TILE = 256


def _dot(a, b):
    return jnp.dot(a, b, preferred_element_type=F32)


def _dot_nt(a, b):
    return lax.dot_general(a, b, (((1,), (1,)), ((), ())), preferred_element_type=F32)


def _params(sem):
    return pltpu.CompilerParams(dimension_semantics=sem, vmem_limit_bytes=VMEM_LIMIT)


def _full(shape):
    nd = len(shape)
    return pl.BlockSpec(shape, lambda *_: (0,) * nd)


def _rms(v, g):
    return v * lax.rsqrt(jnp.mean(v * v, axis=-1, keepdims=True) + EPS) * g


def _split_bf16(v):
    hi = v.astype(BF16)
    lo = (v - hi.astype(F32)).astype(BF16)
    return hi, lo


def _group_matrix(group):
    r = lax.broadcasted_iota(jnp.int32, (LANES, LANES), 0) // group
    c = lax.broadcasted_iota(jnp.int32, (LANES, LANES), 1) // group
    return jnp.where(r == c, 1.0, 0.0).astype(BF16)


def _group_norm_slab(xs, gmat, inv_n):
    hi, lo = _split_bf16(xs * xs)
    ss = _dot(hi, gmat) + _dot(lo, gmat)
    return xs * lax.rsqrt(ss * inv_n + EPS)


def _inproj_kernel(x_ref, g1_ref, wa_ref, wb_ref, wc_ref, wd_ref, wg_ref,
                   gqa_ref, wqb_ref, gkva_ref, wkvk_ref, wkvv_ref,
                   rc_ref, rs1_ref, rs2_ref,
                   gmq_ref, gmk_ref, gsq_ref, gsk_ref, gdq_ref, gdk_ref,
                   sbq, sbk, sbv, mq, mk, mv, swq, swk, swv,
                   dq0, dq1, dq2, dk0, dk1, dk2, dv0, dv1, dv2, glow):
    x = x_ref[...]
    h = _rms(x, g1_ref[...]).astype(BF16)

    pa = _dot(h, wa_ref[...])
    sbq[...] = pa[:, :SB_W].astype(BF16)
    sbk[...] = pa[:, SB_W:2 * SB_W].astype(BF16)
    sbv[...] = pa[:, 2 * SB_W:].astype(BF16)

    pb = _dot(h, wb_ref[...])
    cq = pb[:, :MLA_Q_RANK]
    ckv = pb[:, MLA_Q_RANK:MLA_Q_RANK + MLA_KV_RANK]
    kpe = pb[:, MLA_Q_RANK + MLA_KV_RANK:]
    width = MLA_HEADS * MLA_PAD
    rc = jnp.concatenate([rc_ref[...]] * MLA_HEADS, axis=1)
    rs1 = jnp.concatenate([rs1_ref[...]] * MLA_HEADS, axis=1)
    rs2 = jnp.concatenate([rs2_ref[...]] * MLA_HEADS, axis=1)
    half = MLA_ROPE // 2

    def rope(v):
        return v * rc + pltpu.roll(v, half, 1) * rs1 + pltpu.roll(v, width - half, 1) * rs2

    ones = _group_matrix(LANES)
    q = rope(_dot(_rms(cq, gqa_ref[...]).astype(BF16), wqb_ref[...]))
    kvn = _rms(ckv, gkva_ref[...]).astype(BF16)
    k = _dot(kvn, wkvk_ref[...]) + rope(kpe)
    mv[...] = _dot(kvn, wkvv_ref[...]).astype(BF16)
    gmq = gmq_ref[...]
    gmk = gmk_ref[...]
    for s in range(MLA_HEADS):
        sl = slice(s * LANES, (s + 1) * LANES)
        mq[:, sl] = (_group_norm_slab(q[:, sl], ones, 1.0 / MLA_QK) * gmq[:, sl]).astype(BF16)
        mk[:, sl] = (_group_norm_slab(k[:, sl], ones, 1.0 / MLA_QK) * gmk[:, sl]).astype(BF16)

    g32 = _group_matrix(SW_DIM)
    pc = _dot(h, wc_ref[...])
    gsq = gsq_ref[...]
    gsk = gsk_ref[...]
    for s in range(SW_W // LANES):
        sl = slice(s * LANES, (s + 1) * LANES)
        ks = slice(SW_W + s * LANES, SW_W + (s + 1) * LANES)
        swq[:, sl] = (_group_norm_slab(pc[:, sl], g32, 1.0 / SW_DIM) * gsq[:, sl]).astype(BF16)
        swk[:, sl] = (_group_norm_slab(pc[:, ks], g32, 1.0 / SW_DIM) * gsk[:, sl]).astype(BF16)
    swv[...] = pc[:, 2 * SW_W:].astype(BF16)

    pd = _dot(h, wd_ref[...])
    gdq = gdq_ref[...]
    gdk = gdk_ref[...]
    for g, (oq, ok, ov) in enumerate(((dq0, dk0, dv0), (dq1, dk1, dv1), (dq2, dk2, dv2))):
        sl = slice(g * LANES, (g + 1) * LANES)
        ks = slice(DIL_W + g * LANES, DIL_W + (g + 1) * LANES)
        vs = slice(2 * DIL_W + g * LANES, 2 * DIL_W + (g + 1) * LANES)
        oq[...] = (_group_norm_slab(pd[:, sl], g32, 1.0 / DIL_DIM) * gdq[:, sl]).astype(BF16)
        ok[...] = (_group_norm_slab(pd[:, ks], g32, 1.0 / DIL_DIM) * gdk[:, sl]).astype(BF16)
        ov[...] = pd[:, vs].astype(BF16)

    glow[...] = _dot(h, wg_ref[...]).astype(BF16)


def _inproj(x2, g1, wa, wb, wc, wd, wg, gqa, wqb, gkva, wkvk, wkvv, rc, rs1, rs2,
            gmq, gmk, gsq, gsk, gdq, gdk, seq):
    n = x2.shape[0]
    tm = ROW_TILE
    n_seq_tiles = seq // tm
    row = lambda w: pl.BlockSpec((tm, w), lambda i: (i, 0))
    pos = lambda w: pl.BlockSpec((tm, w), lambda i: (i % n_seq_tiles, 0))
    weights = (g1, wa, wb, wc, wd, wg, gqa, wqb, gkva, wkvk, wkvv)
    gains = (gmq, gmk, gsq, gsk, gdq, gdk)
    out_widths = (SB_W,) * 3 + (MLA_HEADS * MLA_PAD,) * 2 + (MLA_HEADS * MLA_V,) + (SW_W,) * 3 \
        + (DIL_GW,) * 9 + (GATE_RANK,)
    return pl.pallas_call(
        _inproj_kernel,
        grid=(n // tm,),
        in_specs=[row(D_MODEL)] + [_full(w.shape) for w in weights] + [pos(LANES)] * 3
        + [_full(g.shape) for g in gains],
        out_specs=[row(w) for w in out_widths],
        out_shape=[jax.ShapeDtypeStruct((n, w), BF16) for w in out_widths],
        compiler_params=_params(("parallel",)),
        name="inproj",
    )(x2, *weights, rc, rs1, rs2, *gains)


def _sb_kernel(q_ref, k_ref, v_ref, o_ref, acc_ref, car_ref):
    t = ATT_TILE
    i = pl.program_id(2)
    q = q_ref[0]
    lane = lax.broadcasted_iota(jnp.int32, (t, LANES), 1)
    zero = jnp.zeros_like(q)
    qh = (jnp.where(lane < SB_DIM, q, zero), jnp.where(lane >= SB_DIM, q, zero))
    r = lax.broadcasted_iota(jnp.int32, (t, t), 0)
    c = lax.broadcasted_iota(jnp.int32, (t, t), 1)
    suffix = jnp.where(r >= c, 1.0, 0.0).astype(BF16)
    strict = c < r

    acc_ref[...] = jnp.zeros_like(acc_ref)
    car_ref[...] = jnp.zeros_like(car_ref)

    def block(j, diagonal):
        off = pl.multiple_of(j * t, t)
        kb = k_ref[0, pl.ds(off, t), :]
        vb = v_ref[0, pl.ds(off, t), :]
        for h in range(2):
            z = _dot_nt(qh[h], kb)
            lk = -(jnp.maximum(z, 0.0) + jnp.log(1.0 + jnp.exp(-jnp.abs(z))))
            if diagonal:
                lk = jnp.where(strict, lk, 0.0)
            within = _dot(lk.astype(BF16), suffix)
            car = car_ref[h]
            a = jnp.exp(z + within + car)
            if diagonal:
                a = jnp.where(strict, a, 0.0)
            acc_ref[h] += _dot(a.astype(BF16), vb)
            car_ref[h] = car + within[:, :1]

    block(i, True)

    def body(s, carry):
        block(i - 1 - s, False)
        return carry

    lax.fori_loop(0, i, body, 0)
    o_ref[0] = jnp.where(lane < SB_DIM, acc_ref[0], acc_ref[1]).astype(o_ref.dtype)


def _sb_attention(q, k, v):
    b, s, _ = q.shape
    t = ATT_TILE
    return pl.pallas_call(
        _sb_kernel,
        grid=(b, SB_W // LANES, s // t),
        in_specs=[pl.BlockSpec((1, t, LANES), lambda bi, p, i: (bi, i, p)),
                  pl.BlockSpec((1, s, LANES), lambda bi, p, i: (bi, 0, p)),
                  pl.BlockSpec((1, s, LANES), lambda bi, p, i: (bi, 0, p))],
        out_specs=pl.BlockSpec((1, t, LANES), lambda bi, p, i: (bi, i, p)),
        out_shape=jax.ShapeDtypeStruct((b, s, SB_W), BF16),
        scratch_shapes=[pltpu.VMEM((2, t, LANES), F32), pltpu.VMEM((2, t, 1), F32)],
        compiler_params=_params(("parallel", "parallel", "arbitrary")),
        name="stick_breaking",
    )(q, k, v)


def _mla_kernel(q_ref, k_ref, v_ref, o_ref, acc_ref, m_ref, l_ref):
    t = ATT_TILE
    i = pl.program_id(2)
    q = q_ref[0]
    r = lax.broadcasted_iota(jnp.int32, (t, t), 0)
    c = lax.broadcasted_iota(jnp.int32, (t, t), 1)
    causal = c <= r
    lane = lax.broadcasted_iota(jnp.int32, (t, LANES), 1)

    acc_ref[...] = jnp.zeros_like(acc_ref)
    l_ref[...] = jnp.zeros_like(l_ref)
    m_ref[...] = jnp.full_like(m_ref, NEG)

    def block(j, diagonal):
        off = pl.multiple_of(j * t, t)
        kb = k_ref[0, pl.ds(off, t), :]
        vb = v_ref[0, pl.ds(off, t), :]
        for h in range(2):
            sl = slice(h * MLA_PAD, (h + 1) * MLA_PAD)
            s = _dot_nt(q[:, sl], kb[:, sl])
            if diagonal:
                s = jnp.where(causal, s, NEG)
            m_old = m_ref[h]
            m_new = jnp.maximum(m_old, jnp.max(s, axis=-1, keepdims=True))
            alpha = jnp.exp(m_old - m_new)
            p = jnp.exp(s - m_new)
            l_ref[h] = alpha * l_ref[h] + jnp.sum(p, axis=-1, keepdims=True)
            acc_ref[h] = alpha * acc_ref[h] + _dot(p.astype(BF16), vb)
            m_ref[h] = m_new

    block(i, True)

    def body(s, carry):
        block(i - 1 - s, False)
        return carry

    lax.fori_loop(0, i, body, 0)
    o_ref[0] = jnp.where(lane < MLA_V, acc_ref[0] / l_ref[0], acc_ref[1] / l_ref[1]).astype(o_ref.dtype)


def _mla_attention(q, k, v):
    b, s, _ = q.shape
    t = ATT_TILE
    return pl.pallas_call(
        _mla_kernel,
        grid=(b, MLA_HEADS // 2, s // t),
        in_specs=[pl.BlockSpec((1, t, 2 * MLA_PAD), lambda bi, p, i: (bi, i, p)),
                  pl.BlockSpec((1, s, 2 * MLA_PAD), lambda bi, p, i: (bi, 0, p)),
                  pl.BlockSpec((1, s, LANES), lambda bi, p, i: (bi, 0, p))],
        out_specs=pl.BlockSpec((1, t, LANES), lambda bi, p, i: (bi, i, p)),
        out_shape=jax.ShapeDtypeStruct((b, s, MLA_HEADS * MLA_V), BF16),
        scratch_shapes=[pltpu.VMEM((2, t, LANES), F32), pltpu.VMEM((2, t, 1), F32),
                        pltpu.VMEM((2, t, 1), F32)],
        compiler_params=_params(("parallel", "parallel", "arbitrary")),
        name="latent_attention",
    )(q, k, v)


def _band_kernel(*refs, heads, inclusive, with_sinks, with_lse):
    q_ref, kp_ref, kc_ref, vp_ref, vc_ref, bias_ref = refs[:6]
    rest = refs[6:]
    if with_sinks:
        sink_ref, rest = rest[0], rest[1:]
    o_ref = rest[0]
    width = heads * DIL_DIM
    n = pl.program_id(2)
    q = q_ref[0]
    k = jnp.concatenate([kp_ref[0], kc_ref[0]], axis=0)
    v = jnp.concatenate([vp_ref[0], vc_ref[0]], axis=0)
    qi = lax.broadcasted_iota(jnp.int32, (BLOCK, 2 * BLOCK), 0)
    kj = lax.broadcasted_iota(jnp.int32, (BLOCK, 2 * BLOCK), 1)
    near = (kj >= qi) if inclusive else (kj > qi)
    mask = near & (kj <= qi + BLOCK) & ((kj >= BLOCK) | (n > 0))
    group = lax.broadcasted_iota(jnp.int32, (BLOCK, width), 1) // DIL_DIM
    zero = jnp.zeros_like(q)
    out = jnp.zeros((BLOCK, width), F32)
    lse = jnp.zeros((BLOCK, width), F32)
    for h in range(heads):
        s = _dot_nt(jnp.where(group == h, q, zero), k) + bias_ref[h]
        s = jnp.where(mask, s, NEG)
        m = jnp.max(s, axis=-1, keepdims=True)
        if with_sinks:
            sink = sink_ref[h]
            m = jnp.maximum(m, sink)
        p = jnp.exp(s - m)
        l = jnp.sum(p, axis=-1, keepdims=True)
        if with_sinks:
            l = l + jnp.exp(sink - m)
        o = _dot(p.astype(BF16), v) / l
        out = jnp.where(group == h, o, out)
        if with_lse:
            lse = jnp.where(group == h, m + jnp.log(l), lse)
    o_ref[0] = out.astype(o_ref.dtype)
    if with_lse:
        rest[1][0] = lse


def _band_attention(q, k, v, bias, sinks, dil, heads, inclusive, with_lse, out_dtype):
    b, s, width = q.shape
    m = s // dil
    nb = m // BLOCK
    view = lambda a: a.reshape(b, m, dil * width)
    cur = pl.BlockSpec((1, BLOCK, width), lambda bi, r, n: (bi, n, r))
    prev = pl.BlockSpec((1, BLOCK, width), lambda bi, r, n: (bi, jnp.maximum(n - 1, 0), r))
    in_specs = [cur, prev, cur, prev, cur, _full(bias.shape)]
    args = [view(q), view(k), view(k), view(v), view(v), bias]
    if sinks is not None:
        in_specs.append(pl.BlockSpec(memory_space=pltpu.SMEM))
        args.append(sinks)
    out_shape = [jax.ShapeDtypeStruct((b, m, dil * width), out_dtype)]
    out_specs = [cur]
    if with_lse:
        out_shape.append(jax.ShapeDtypeStruct((b, m, dil * width), F32))
        out_specs.append(cur)
    outs = pl.pallas_call(
        functools.partial(_band_kernel, heads=heads, inclusive=inclusive,
                          with_sinks=sinks is not None, with_lse=with_lse),
        grid=(b, dil, nb),
        in_specs=in_specs,
        out_specs=out_specs,
        out_shape=out_shape,
        compiler_params=_params(("parallel", "parallel", "arbitrary")),
        name="band_attention",
    )(*args)
    return [o.reshape(b, s, width) for o in outs]


def _bias_kernel(rb_ref, bkt_ref, o_ref, *, head_patterns):
    for head, pat in enumerate(head_patterns):
        bkt = bkt_ref[pat]
        acc = jnp.zeros(bkt.shape, F32)
        for bucket in range(REL_BUCKETS):
            acc = jnp.where(bkt == bucket, rb_ref[bucket, head], acc)
        o_ref[head] = acc


def _t5_bucket(dist):
    max_exact = REL_BUCKETS // 2
    d = jnp.maximum(dist, 1).astype(F32)
    large = max_exact + (jnp.log(d / max_exact) / math.log(REL_MAX_DIST / max_exact)
                         * (REL_BUCKETS - max_exact)).astype(jnp.int32)
    large = jnp.minimum(large, REL_BUCKETS - 1)
    return jnp.where(dist < max_exact, dist, large)


def _bias_tables(rel_bias):
    dist = (BLOCK + jnp.arange(BLOCK))[:, None] - jnp.arange(2 * BLOCK)[None, :]
    dist = jnp.maximum(dist, 0)
    buckets = jnp.stack([_t5_bucket(dist * dil) for _, dil in DIL_PATTERNS]).astype(jnp.int32)
    head_patterns = (0,) * SW_Q_HEADS + tuple(g for g in range(N_DIL) for _ in range(DIL_HEADS))
    n_heads = len(head_patterns)
    return pl.pallas_call(
        functools.partial(_bias_kernel, head_patterns=head_patterns),
        in_specs=[pl.BlockSpec(memory_space=pltpu.SMEM), _full(buckets.shape)],
        out_specs=_full((n_heads, BLOCK, 2 * BLOCK)),
        out_shape=jax.ShapeDtypeStruct((n_heads, BLOCK, 2 * BLOCK), F32),
        grid=(1,),
        name="rel_bias",
    )(rel_bias, buckets)


def _merge_kernel(*refs, with_router):
    (x_ref, glow_ref, oa_ref, ob_ref, oc_ref, d0_ref, d1_ref, d2_ref, l0_ref, l1_ref, l2_ref,
     wgb_ref, bg_ref, wbr_ref, wout_ref, g2_ref) = refs[:16]
    rest = refs[16:]
    if with_router:
        wr_hi_ref, wr_lo_ref, br_ref = rest[:3]
        rest = rest[3:]
    x1_ref, h2_ref = rest[:2]

    l0, l1, l2 = l0_ref[...], l1_ref[...], l2_ref[...]
    mx = jnp.maximum(jnp.maximum(l0, l1), l2)
    e0, e1, e2 = jnp.exp(l0 - mx), jnp.exp(l1 - mx), jnp.exp(l2 - mx)
    od = (e0 * d0_ref[...] + e1 * d1_ref[...] + e2 * d2_ref[...]) / (e0 + e1 + e2)

    glow = glow_ref[...]
    branches = (oa_ref[...], ob_ref[...], oc_ref[...], od.astype(BF16))
    y = None
    row = 0
    for i, o in enumerate(branches):
        cols = slice(i * D_MODEL, (i + 1) * D_MODEL)
        pre = _dot(glow, wgb_ref[:, cols]) + bg_ref[:, cols]
        gate = 1.0 / (1.0 + jnp.exp(-pre))
        term = gate * _dot(o, wbr_ref[row:row + BR_SIZES[i], :])
        y = term if y is None else y + term
        row += BR_SIZES[i]
    x1 = x_ref[...] + _dot(y.astype(BF16), wout_ref[...])
    x1_ref[...] = x1
    h2 = _rms(x1, g2_ref[...])
    h2_ref[...] = h2.astype(BF16)

    if with_router:
        wts_ref = rest[2]
        hi, lo = _split_bf16(h2)
        whi = wr_hi_ref[...]
        lg = _dot(hi, whi) + _dot(lo, whi) + _dot(hi, wr_lo_ref[...]) + br_ref[...]
        lane = lax.broadcasted_iota(jnp.int32, lg.shape, 1).astype(F32)
        big = float(LANES)
        m1 = jnp.max(lg, axis=-1, keepdims=True)
        i1 = jnp.min(jnp.where(lg == m1, lane, big), axis=-1, keepdims=True)
        lg2 = jnp.where(lane == i1, NEG, lg)
        m2 = jnp.max(lg2, axis=-1, keepdims=True)
        i2 = jnp.min(jnp.where(lg2 == m2, lane, big), axis=-1, keepdims=True)
        e = jnp.exp(m2 - m1)
        w1 = 1.0 / (1.0 + e)
        w2 = e / (1.0 + e)
        wts_ref[...] = jnp.where(lane == i1, w1, 0.0) + jnp.where(lane == i2, w2, 0.0)


def _merge(x2, glow, oa, ob, oc, dil_o, dil_l, wgb, bg, wbr, wout, g2, router):
    n = x2.shape[0]
    tm = ROW_TILE
    row = lambda w: pl.BlockSpec((tm, w), lambda i: (i, 0))
    acts = [x2, glow, oa, ob, oc, *dil_o, *dil_l]
    weights = [wgb, bg, wbr, wout, g2] + (list(router) if router is not None else [])
    out_shape = [jax.ShapeDtypeStruct((n, D_MODEL), F32), jax.ShapeDtypeStruct((n, D_MODEL), BF16)]
    out_specs = [row(D_MODEL), row(D_MODEL)]
    if router is not None:
        out_shape.append(jax.ShapeDtypeStruct((n, LANES), F32))
        out_specs.append(row(LANES))
    return pl.pallas_call(
        functools.partial(_merge_kernel, with_router=router is not None),
        grid=(n // tm,),
        in_specs=[row(a.shape[1]) for a in acts] + [_full(w.shape) for w in weights],
        out_specs=out_specs,
        out_shape=out_shape,
        compiler_params=_params(("parallel",)),
        name="merge",
    )(*acts, *weights)


FF_CHUNK = 512


def _dense_ffn_kernel(x1_ref, h2_ref, wg_ref, wu_ref, wd_ref, o_ref):
    h2 = h2_ref[...]
    acc = x1_ref[...]
    for c in range(D_FF_DENSE // FF_CHUNK):
        cols = slice(c * FF_CHUNK, (c + 1) * FF_CHUNK)
        g = _dot(h2, wg_ref[:, cols])
        u = _dot(h2, wu_ref[:, cols])
        act = (g / (1.0 + jnp.exp(-g)) * u).astype(BF16)
        acc = acc + _dot(act, wd_ref[cols, :])
    o_ref[...] = acc


def _dense_ffn(x1, h2, wg, wu, wd):
    n = x1.shape[0]
    tm = ROW_TILE
    row = pl.BlockSpec((tm, D_MODEL), lambda i: (i, 0))
    return pl.pallas_call(
        _dense_ffn_kernel,
        grid=(n // tm,),
        in_specs=[row, row, _full(wg.shape), _full(wu.shape), _full(wd.shape)],
        out_specs=row,
        out_shape=jax.ShapeDtypeStruct((n, D_MODEL), F32),
        compiler_params=_params(("parallel",)),
        name="dense_ffn",
    )(x1, h2, wg, wu, wd)


def _moe_kernel(x1_ref, h2_ref, wts_ref, wg_ref, wu_ref, wd_ref, o_ref, acc_ref):
    e = pl.program_id(1)

    @pl.when(e == 0)
    def _():
        acc_ref[...] = x1_ref[...]

    wts = wts_ref[...]
    lane = lax.broadcasted_iota(jnp.int32, wts.shape, 1)
    w_e = jnp.sum(jnp.where(lane == e, wts, 0.0), axis=-1, keepdims=True)
    h2 = h2_ref[...]
    g = _dot(h2, wg_ref[0])
    u = _dot(h2, wu_ref[0])
    act = (g / (1.0 + jnp.exp(-g)) * u).astype(BF16)
    acc_ref[...] += w_e * _dot(act, wd_ref[0])

    @pl.when(e == N_EXPERTS - 1)
    def _():
        o_ref[...] = acc_ref[...]


def _moe_ffn(x1, h2, wts, wg, wu, wd):
    n = x1.shape[0]
    tm = ROW_TILE
    row = lambda w: pl.BlockSpec((tm, w), lambda i, e: (i, 0))
    return pl.pallas_call(
        _moe_kernel,
        grid=(n // tm, N_EXPERTS),
        in_specs=[row(D_MODEL), row(D_MODEL), row(LANES),
                  pl.BlockSpec((1, D_MODEL, D_FF_EXPERT), lambda i, e: (e, 0, 0)),
                  pl.BlockSpec((1, D_MODEL, D_FF_EXPERT), lambda i, e: (e, 0, 0)),
                  pl.BlockSpec((1, D_FF_EXPERT, D_MODEL), lambda i, e: (e, 0, 0))],
        out_specs=row(D_MODEL),
        out_shape=jax.ShapeDtypeStruct((n, D_MODEL), F32),
        scratch_shapes=[pltpu.VMEM((tm, D_MODEL), F32)],
        compiler_params=_params(("parallel", "arbitrary")),
        name="moe_ffn",
    )(x1, h2, wts, wg, wu, wd)


def _rope_tables(seq):
    half = MLA_ROPE // 2
    inv = ROPE_THETA ** (-jnp.arange(half, dtype=F32) / half)
    ang = jnp.arange(seq, dtype=F32)[:, None] * inv[None, :]
    cos, sin = jnp.cos(ang), jnp.sin(ang)
    zeros = lambda w: jnp.zeros((seq, w), F32)
    tail = LANES - MLA_QK
    rc = jnp.concatenate([jnp.ones((seq, MLA_NOPE), F32), cos, cos, zeros(tail)], axis=1)
    rs1 = jnp.concatenate([zeros(MLA_NOPE + half), sin, zeros(tail)], axis=1)
    rs2 = jnp.concatenate([zeros(MLA_NOPE), -sin, zeros(half + tail)], axis=1)
    return rc, rs1, rs2


def _pad_heads(w, heads, dim):
    rows = w.shape[0]
    w = w.reshape(rows, heads, dim)
    return jnp.pad(w, ((0, 0), (0, 0), (0, LANES - dim))).reshape(rows, heads * LANES)


def _layer_weights(w_in, g_qa, w_qb, g_kva, w_kvb, qk_g_mla, qk_g_sw, qk_g_dil, w_gate_a):
    offs = np.concatenate([[0], np.cumsum(IN_SIZES)]).tolist()
    cols = [w_in[:, offs[j]:offs[j + 1]] for j in range(len(IN_SIZES))]
    a_q, a_k, a_v, b_cq, b_ckv, b_kpe, c_q, c_k, c_v, d_q, d_k, d_v = cols
    wa = jnp.concatenate([a_q * (SB_DIM ** -0.5), a_k, a_v], axis=1)
    kpe_pad = jnp.pad(b_kpe, ((0, 0), (MLA_NOPE, LANES - MLA_QK)))
    wb = jnp.concatenate([b_cq, b_ckv] + [kpe_pad] * MLA_HEADS, axis=1)
    rep = SW_Q_HEADS // SW_KV_HEADS
    expand = lambda w: jnp.repeat(w.reshape(D_MODEL, SW_KV_HEADS, SW_DIM), rep, axis=1).reshape(D_MODEL, SW_W)
    wc = jnp.concatenate([c_q, expand(c_k), expand(c_v)], axis=1)
    wd = jnp.concatenate([d_q, d_k, d_v], axis=1)
    wqb = _pad_heads(w_qb, MLA_HEADS, MLA_QK)
    kvb = w_kvb.reshape(MLA_KV_RANK, MLA_HEADS, MLA_NOPE + MLA_V)
    wkvk = _pad_heads(kvb[:, :, :MLA_NOPE].reshape(MLA_KV_RANK, -1), MLA_HEADS, MLA_NOPE)
    wkvv = kvb[:, :, MLA_NOPE:].reshape(MLA_KV_RANK, MLA_HEADS * MLA_V)
    pad_gain = lambda g: jnp.tile(jnp.pad(g, (0, LANES - MLA_QK)), MLA_HEADS)[None, :]
    gmq = pad_gain(qk_g_mla[0]) * (MLA_QK ** -0.5)
    gmk = pad_gain(qk_g_mla[1])
    gsq = jnp.tile(qk_g_sw[0], SW_Q_HEADS)[None, :] * (SW_DIM ** -0.5)
    gsk = jnp.tile(qk_g_sw[1], SW_Q_HEADS)[None, :]
    gdq = jnp.tile(qk_g_dil[0], N_DIL * DIL_HEADS)[None, :] * (DIL_DIM ** -0.5)
    gdk = jnp.tile(qk_g_dil[1], N_DIL * DIL_HEADS)[None, :]
    bf = lambda w: w.astype(BF16)
    return (bf(wa), bf(wb), bf(wc), bf(wd), bf(w_gate_a), g_qa[None, :], bf(wqb), g_kva[None, :],
            bf(wkvk), bf(wkvv)), (gmq, gmk, gsq, gsk, gdq, gdk)


def kernel(x, norm1_g, w_in, g_qa, w_qb, g_kva, w_kvb, qk_g_mla, qk_g_sw, qk_g_dil, sinks, rel_bias, w_gate_a, w_gate_b, b_gate, w_branch, w_out, norm2_g, w_gu_dense, w_down_dense, w_router, b_router, w_gu_exp, w_down_exp):
    b, s, d = x.shape
    n = b * s
    depth = norm1_g.shape[0]
    bias = _bias_tables(rel_bias)
    sw_bias = bias[:SW_Q_HEADS]
    dil_bias = [bias[SW_Q_HEADS + g * DIL_HEADS:SW_Q_HEADS + (g + 1) * DIL_HEADS] for g in range(N_DIL)]
    rc, rs1, rs2 = _rope_tables(s)
    x2 = x.reshape(n, d)
    seq3 = lambda a: a.reshape(b, s, a.shape[-1])
    flat = lambda a: a.reshape(n, a.shape[-1])

    for i in range(depth):
        weights, gains = _layer_weights(w_in[i], g_qa[i], w_qb[i], g_kva[i], w_kvb[i], qk_g_mla[i],
                                        qk_g_sw[i], qk_g_dil[i], w_gate_a[i])
        wa, wb, wc, wd, wg, gqa, wqb, gkva, wkvk, wkvv = weights
        (sbq, sbk, sbv, mq, mk, mv, swq, swk, swv,
         dq0, dq1, dq2, dk0, dk1, dk2, dv0, dv1, dv2, glow) = _inproj(
            x2, norm1_g[i][None, :], wa, wb, wc, wd, wg, gqa, wqb, gkva, wkvk, wkvv, rc, rs1, rs2,
            *gains, s)

        out_a = _sb_attention(seq3(sbq), seq3(sbk), seq3(sbv))
        out_b = _mla_attention(seq3(mq), seq3(mk), seq3(mv))
        (out_c,) = _band_attention(seq3(swq), seq3(swk), seq3(swv), sw_bias, sinks[i], 1,
                                   SW_Q_HEADS, False, False, BF16)
        dil_o, dil_l = [], []
        for g, (dq, dk, dv) in enumerate(((dq0, dk0, dv0), (dq1, dk1, dv1), (dq2, dk2, dv2))):
            o, l = _band_attention(seq3(dq), seq3(dk), seq3(dv), dil_bias[g], None,
                                   DIL_PATTERNS[g][1], DIL_HEADS, True, True, F32)
            dil_o.append(flat(o))
            dil_l.append(flat(l))

        router = None
        if i % 2 == 1:
            wr = jnp.pad(w_router[i // 2], ((0, 0), (0, LANES - N_EXPERTS)))
            wr_hi = wr.astype(BF16)
            wr_lo = (wr - wr_hi.astype(F32)).astype(BF16)
            br = jnp.pad(b_router[i // 2], (0, LANES - N_EXPERTS), constant_values=NEG)[None, :]
            router = (wr_hi, wr_lo, br)
        merged = _merge(x2, glow, flat(out_a), flat(out_b), flat(out_c), dil_o, dil_l,
                        w_gate_b[i].astype(BF16), b_gate[i][None, :], w_branch[i].astype(BF16),
                        w_out[i].astype(BF16), norm2_g[i][None, :], router)
        if i % 2 == 0:
            x1, h2 = merged
            w_gu = w_gu_dense[i // 2].astype(BF16)
            x2 = _dense_ffn(x1, h2, w_gu[:, :D_FF_DENSE], w_gu[:, D_FF_DENSE:],
                            w_down_dense[i // 2].astype(BF16))
        else:
            x1, h2, wts = merged
            w_gu = w_gu_exp[i // 2].astype(BF16)
            x2 = _moe_ffn(x1, h2, wts, w_gu[:, :, :D_FF_EXPERT], w_gu[:, :, D_FF_EXPERT:],
                          w_down_exp[i // 2].astype(BF16))
    return x2.reshape(b, s, d)
```

```python
import functools
import math

import jax
import jax.numpy as jnp
import numpy as np
from jax import lax
from jax.experimental import pallas as pl
from jax.experimental.pallas import tpu as pltpu

F32 = jnp.float32
BF16 = jnp.bfloat16

D_MODEL = 1024
BLOCK = 128
EPS = 1e-6
SB_HEADS, SB_DIM = 4, 64
MLA_HEADS, MLA_Q_RANK, MLA_KV_RANK = 4, 256, 256
MLA_NOPE, MLA_ROPE, MLA_V = 64, 32, 64
MLA_QK = MLA_NOPE + MLA_ROPE
ROPE_THETA = 10000.0
SW_Q_HEADS, SW_KV_HEADS, SW_DIM = 8, 2, 32
DIL_PATTERNS = ((128, 1), (512, 4), (2048, 16))
DIL_HEADS, DIL_DIM = 4, 32
N_DIL = len(DIL_PATTERNS)
REL_BUCKETS, REL_MAX_DIST = 32, 2048
GATE_RANK, N_BRANCH = 128, 4
D_FF_DENSE = 2048
N_EXPERTS, TOP_K, D_FF_EXPERT = 8, 2, 768

SB_W = SB_HEADS * SB_DIM
SW_W = SW_Q_HEADS * SW_DIM
SW_KV_W = SW_KV_HEADS * SW_DIM
DIL_GW = DIL_HEADS * DIL_DIM
DIL_W = N_DIL * DIL_GW
IN_SIZES = (SB_W, SB_W, SB_W, MLA_Q_RANK, MLA_KV_RANK, MLA_ROPE, SW_W, SW_KV_W, SW_KV_W, DIL_W, DIL_W, DIL_W)
BR_SIZES = (SB_W, MLA_HEADS * MLA_V, SW_W, DIL_GW)

LANES = 128
MLA_PAD = LANES
NEG = -1e30
VMEM_LIMIT = 56 * 1024 * 1024

ROW_TILE = 512
ATT_TILE = 256
MLA_TILE = 512
SB_LOG_FLOOR = -105.0


def _dot(a, b):
    return jnp.dot(a, b, preferred_element_type=F32)


def _dot_nt(a, b):
    return lax.dot_general(a, b, (((1,), (1,)), ((), ())), preferred_element_type=F32)


def _params(sem):
    return pltpu.CompilerParams(dimension_semantics=sem, vmem_limit_bytes=VMEM_LIMIT)


def _full(shape):
    nd = len(shape)
    return pl.BlockSpec(shape, lambda *_: (0,) * nd)


def _rms(v, g):
    return v * lax.rsqrt(jnp.mean(v * v, axis=-1, keepdims=True) + EPS) * g


def _split_bf16(v):
    hi = v.astype(BF16)
    lo = (v - hi.astype(F32)).astype(BF16)
    return hi, lo


def _group_matrix(group):
    r = lax.broadcasted_iota(jnp.int32, (LANES, LANES), 0) // group
    c = lax.broadcasted_iota(jnp.int32, (LANES, LANES), 1) // group
    return jnp.where(r == c, 1.0, 0.0).astype(BF16)


def _group_norm_slab(xs, gmat, inv_n):
    hi, lo = _split_bf16(xs * xs)
    ss = _dot(hi, gmat) + _dot(lo, gmat)
    return xs * lax.rsqrt(ss * inv_n + EPS)


def _inproj_kernel(x_ref, g1_ref, wa_ref, wb_ref, wc_ref, wd_ref, wg_ref,
                   gqa_ref, wqb_ref, gkva_ref, wkvk_ref, wkvv_ref,
                   rc_ref, rs1_ref, rs2_ref,
                   gmq_ref, gmk_ref, gsq_ref, gsk_ref, gdq_ref, gdk_ref,
                   sbq, sbk, sbv, mq, mk, mv, swq, swk, swv,
                   dq0, dq1, dq2, dk0, dk1, dk2, dv0, dv1, dv2, glow):
    x = x_ref[...]
    h = _rms(x, g1_ref[...]).astype(BF16)

    pa = _dot(h, wa_ref[...])
    sbq[...] = pa[:, :SB_W].astype(BF16)
    sbk[...] = pa[:, SB_W:2 * SB_W].astype(BF16)
    sbv[...] = pa[:, 2 * SB_W:].astype(BF16)

    pb = _dot(h, wb_ref[...])
    cq = pb[:, :MLA_Q_RANK]
    ckv = pb[:, MLA_Q_RANK:MLA_Q_RANK + MLA_KV_RANK]
    kpe = pb[:, MLA_Q_RANK + MLA_KV_RANK:]
    width = MLA_HEADS * MLA_PAD
    rc = jnp.concatenate([rc_ref[...]] * MLA_HEADS, axis=1)
    rs1 = jnp.concatenate([rs1_ref[...]] * MLA_HEADS, axis=1)
    rs2 = jnp.concatenate([rs2_ref[...]] * MLA_HEADS, axis=1)
    half = MLA_ROPE // 2

    def rope(v):
        return v * rc + pltpu.roll(v, half, 1) * rs1 + pltpu.roll(v, width - half, 1) * rs2

    ones = _group_matrix(LANES)
    q = rope(_dot(_rms(cq, gqa_ref[...]).astype(BF16), wqb_ref[...]))
    kvn = _rms(ckv, gkva_ref[...]).astype(BF16)
    k = _dot(kvn, wkvk_ref[...]) + rope(kpe)
    mv[...] = _dot(kvn, wkvv_ref[...]).astype(BF16)
    gmq = gmq_ref[...]
    gmk = gmk_ref[...]
    for s in range(MLA_HEADS):
        sl = slice(s * LANES, (s + 1) * LANES)
        mq[:, sl] = (_group_norm_slab(q[:, sl], ones, 1.0 / MLA_QK) * gmq[:, sl]).astype(BF16)
        mk[:, sl] = (_group_norm_slab(k[:, sl], ones, 1.0 / MLA_QK) * gmk[:, sl]).astype(BF16)

    g32 = _group_matrix(SW_DIM)
    pc = _dot(h, wc_ref[...])
    gsq = gsq_ref[...]
    gsk = gsk_ref[...]
    for s in range(SW_W // LANES):
        sl = slice(s * LANES, (s + 1) * LANES)
        ks = slice(SW_W + s * LANES, SW_W + (s + 1) * LANES)
        swq[:, sl] = (_group_norm_slab(pc[:, sl], g32, 1.0 / SW_DIM) * gsq[:, sl]).astype(BF16)
        swk[:, sl] = (_group_norm_slab(pc[:, ks], g32, 1.0 / SW_DIM) * gsk[:, sl]).astype(BF16)
    swv[...] = pc[:, 2 * SW_W:].astype(BF16)

    pd = _dot(h, wd_ref[...])
    gdq = gdq_ref[...]
    gdk = gdk_ref[...]
    for g, (oq, ok, ov) in enumerate(((dq0, dk0, dv0), (dq1, dk1, dv1), (dq2, dk2, dv2))):
        sl = slice(g * LANES, (g + 1) * LANES)
        ks = slice(DIL_W + g * LANES, DIL_W + (g + 1) * LANES)
        vs = slice(2 * DIL_W + g * LANES, 2 * DIL_W + (g + 1) * LANES)
        oq[...] = (_group_norm_slab(pd[:, sl], g32, 1.0 / DIL_DIM) * gdq[:, sl]).astype(BF16)
        ok[...] = (_group_norm_slab(pd[:, ks], g32, 1.0 / DIL_DIM) * gdk[:, sl]).astype(BF16)
        ov[...] = pd[:, vs].astype(BF16)

    glow[...] = _dot(h, wg_ref[...]).astype(BF16)


def _inproj(x2, g1, wa, wb, wc, wd, wg, gqa, wqb, gkva, wkvk, wkvv, rc, rs1, rs2,
            gmq, gmk, gsq, gsk, gdq, gdk, seq):
    n = x2.shape[0]
    tm = ROW_TILE
    n_seq_tiles = seq // tm
    row = lambda w: pl.BlockSpec((tm, w), lambda i: (i, 0))
    pos = lambda w: pl.BlockSpec((tm, w), lambda i: (i % n_seq_tiles, 0))
    weights = (g1, wa, wb, wc, wd, wg, gqa, wqb, gkva, wkvk, wkvv)
    gains = (gmq, gmk, gsq, gsk, gdq, gdk)
    out_widths = (SB_W,) * 3 + (MLA_HEADS * MLA_PAD,) * 2 + (MLA_HEADS * MLA_V,) + (SW_W,) * 3 \
        + (DIL_GW,) * 9 + (GATE_RANK,)
    return pl.pallas_call(
        _inproj_kernel,
        grid=(n // tm,),
        in_specs=[row(D_MODEL)] + [_full(w.shape) for w in weights] + [pos(LANES)] * 3
        + [_full(g.shape) for g in gains],
        out_specs=[row(w) for w in out_widths],
        out_shape=[jax.ShapeDtypeStruct((n, w), BF16) for w in out_widths],
        compiler_params=_params(("parallel",)),
        name="inproj",
    )(x2, *weights, rc, rs1, rs2, *gains)


def _sb_kernel(q_ref, k_ref, v_ref, o_ref, acc_ref, car_ref):
    t = ATT_TILE
    i = pl.program_id(2)
    q = q_ref[0]
    lane = lax.broadcasted_iota(jnp.int32, (t, LANES), 1)
    zero = jnp.zeros_like(q)
    qh = (jnp.where(lane < SB_DIM, q, zero), jnp.where(lane >= SB_DIM, q, zero))
    r = lax.broadcasted_iota(jnp.int32, (t, t), 0)
    c = lax.broadcasted_iota(jnp.int32, (t, t), 1)
    suffix = jnp.where(r >= c, 1.0, 0.0).astype(BF16)
    strict = c < r

    acc_ref[...] = jnp.zeros_like(acc_ref)
    car_ref[...] = jnp.zeros_like(car_ref)

    def block(j, diagonal):
        off = pl.multiple_of(j * t, t)
        kb = k_ref[0, pl.ds(off, t), :]
        vb = v_ref[0, pl.ds(off, t), :]
        for h in range(2):
            z = _dot_nt(qh[h], kb)
            lk = -(jnp.maximum(z, 0.0) + jnp.log(1.0 + jnp.exp(-jnp.abs(z))))
            if diagonal:
                lk = jnp.where(strict, lk, 0.0)
            within = _dot(lk.astype(BF16), suffix)
            car = car_ref[h]
            a = jnp.exp(jnp.minimum(z + within, 0.0) + car)
            if diagonal:
                a = jnp.where(strict, a, 0.0)
            acc_ref[h] += _dot(a.astype(BF16), vb)
            car_ref[h] = car + within[:, :1]

    block(i, True)

    def live(s):
        return jnp.logical_and(s < i, jnp.max(car_ref[...]) > SB_LOG_FLOOR)

    def body(s):
        block(i - 1 - s, False)
        return s + 1

    lax.while_loop(live, body, 0)
    o_ref[0] = jnp.where(lane < SB_DIM, acc_ref[0], acc_ref[1]).astype(o_ref.dtype)


def _sb_attention(q, k, v):
    b, s, _ = q.shape
    t = ATT_TILE
    return pl.pallas_call(
        _sb_kernel,
        grid=(b, SB_W // LANES, s // t),
        in_specs=[pl.BlockSpec((1, t, LANES), lambda bi, p, i: (bi, i, p)),
                  pl.BlockSpec((1, s, LANES), lambda bi, p, i: (bi, 0, p)),
                  pl.BlockSpec((1, s, LANES), lambda bi, p, i: (bi, 0, p))],
        out_specs=pl.BlockSpec((1, t, LANES), lambda bi, p, i: (bi, i, p)),
        out_shape=jax.ShapeDtypeStruct((b, s, SB_W), BF16),
        scratch_shapes=[pltpu.VMEM((2, t, LANES), F32), pltpu.VMEM((2, t, 1), F32)],
        compiler_params=_params(("parallel", "parallel", "arbitrary")),
        name="stick_breaking",
    )(q, k, v)


def _mla_kernel(q_ref, k_ref, v_ref, o_ref, acc_ref, m_ref, l_ref):
    t = MLA_TILE
    i = pl.program_id(2)
    q = q_ref[0]
    r = lax.broadcasted_iota(jnp.int32, (t, t), 0)
    c = lax.broadcasted_iota(jnp.int32, (t, t), 1)
    causal = c <= r
    lane = lax.broadcasted_iota(jnp.int32, (t, LANES), 1)
    reps = t // LANES

    acc_ref[...] = jnp.zeros_like(acc_ref)
    l_ref[...] = jnp.zeros_like(l_ref)
    m_ref[...] = jnp.full_like(m_ref, NEG)

    def block(j, diagonal):
        off = pl.multiple_of(j * t, t)
        kb = k_ref[0, pl.ds(off, t), :]
        vb = v_ref[0, pl.ds(off, t), :]
        for h in range(2):
            sl = slice(h * MLA_PAD, (h + 1) * MLA_PAD)
            s = _dot_nt(q[:, sl], kb[:, sl])
            if diagonal:
                s = jnp.where(causal, s, NEG)
            m_old = m_ref[h]
            m_new = jnp.maximum(m_old, jnp.max(s, axis=-1, keepdims=True))
            alpha = jnp.exp2(m_old - m_new)
            p = jnp.exp2(s - pltpu.repeat(m_new, reps, 1))
            l_ref[h] = alpha * l_ref[h] + jnp.sum(p, axis=-1, keepdims=True)
            acc_ref[h] = alpha * acc_ref[h] + _dot(p.astype(BF16), vb)
            m_ref[h] = m_new

    block(i, True)

    def body(s, carry):
        block(i - 1 - s, False)
        return carry

    lax.fori_loop(0, i, body, 0)
    o_ref[0] = jnp.where(lane < MLA_V, acc_ref[0] / l_ref[0], acc_ref[1] / l_ref[1]).astype(o_ref.dtype)


def _mla_attention(q, k, v):
    b, s, _ = q.shape
    t = MLA_TILE
    stat = pltpu.VMEM((2, t, LANES), F32)
    return pl.pallas_call(
        _mla_kernel,
        grid=(b, MLA_HEADS // 2, s // t),
        in_specs=[pl.BlockSpec((1, t, 2 * MLA_PAD), lambda bi, p, i: (bi, i, p)),
                  pl.BlockSpec((1, s, 2 * MLA_PAD), lambda bi, p, i: (bi, 0, p)),
                  pl.BlockSpec((1, s, LANES), lambda bi, p, i: (bi, 0, p))],
        out_specs=pl.BlockSpec((1, t, LANES), lambda bi, p, i: (bi, i, p)),
        out_shape=jax.ShapeDtypeStruct((b, s, MLA_HEADS * MLA_V), BF16),
        scratch_shapes=[stat, stat, stat],
        compiler_params=_params(("parallel", "parallel", "arbitrary")),
        name="latent_attention",
    )(q, k, v)


def _band_kernel(*refs, heads, inclusive, with_sinks, with_lse):
    q_ref, kp_ref, kc_ref, vp_ref, vc_ref, bias_ref = refs[:6]
    rest = refs[6:]
    if with_sinks:
        sink_ref, rest = rest[0], rest[1:]
    o_ref = rest[0]
    width = heads * DIL_DIM
    n = pl.program_id(2)
    q = q_ref[0]
    k = jnp.concatenate([kp_ref[0], kc_ref[0]], axis=0)
    v = jnp.concatenate([vp_ref[0], vc_ref[0]], axis=0)
    qi = lax.broadcasted_iota(jnp.int32, (BLOCK, 2 * BLOCK), 0)
    kj = lax.broadcasted_iota(jnp.int32, (BLOCK, 2 * BLOCK), 1)
    near = (kj >= qi) if inclusive else (kj > qi)
    mask = near & (kj <= qi + BLOCK) & ((kj >= BLOCK) | (n > 0))
    group = lax.broadcasted_iota(jnp.int32, (BLOCK, width), 1) // DIL_DIM
    zero = jnp.zeros_like(q)
    out = jnp.zeros((BLOCK, width), F32)
    lse = jnp.zeros((BLOCK, width), F32)
    for h in range(heads):
        s = _dot_nt(jnp.where(group == h, q, zero), k) + bias_ref[h]
        s = jnp.where(mask, s, NEG)
        m = jnp.max(s, axis=-1, keepdims=True)
        if with_sinks:
            sink = sink_ref[h]
            m = jnp.maximum(m, sink)
        p = jnp.exp(s - m)
        l = jnp.sum(p, axis=-1, keepdims=True)
        if with_sinks:
            l = l + jnp.exp(sink - m)
        o = _dot(p.astype(BF16), v) / l
        out = jnp.where(group == h, o, out)
        if with_lse:
            lse = jnp.where(group == h, m + jnp.log(l), lse)
    o_ref[0] = out.astype(o_ref.dtype)
    if with_lse:
        rest[1][0] = lse


def _band_attention(q, k, v, bias, sinks, dil, heads, inclusive, with_lse, out_dtype):
    b, s, width = q.shape
    m = s // dil
    nb = m // BLOCK
    view = lambda a: a.reshape(b, m, dil * width)
    cur = pl.BlockSpec((1, BLOCK, width), lambda bi, r, n: (bi, n, r))
    prev = pl.BlockSpec((1, BLOCK, width), lambda bi, r, n: (bi, jnp.maximum(n - 1, 0), r))
    in_specs = [cur, prev, cur, prev, cur, _full(bias.shape)]
    args = [view(q), view(k), view(k), view(v), view(v), bias]
    if sinks is not None:
        in_specs.append(pl.BlockSpec(memory_space=pltpu.SMEM))
        args.append(sinks)
    out_shape = [jax.ShapeDtypeStruct((b, m, dil * width), out_dtype)]
    out_specs = [cur]
    if with_lse:
        out_shape.append(jax.ShapeDtypeStruct((b, m, dil * width), F32))
        out_specs.append(cur)
    outs = pl.pallas_call(
        functools.partial(_band_kernel, heads=heads, inclusive=inclusive,
                          with_sinks=sinks is not None, with_lse=with_lse),
        grid=(b, dil, nb),
        in_specs=in_specs,
        out_specs=out_specs,
        out_shape=out_shape,
        compiler_params=_params(("parallel", "parallel", "arbitrary")),
        name="band_attention",
    )(*args)
    return [o.reshape(b, s, width) for o in outs]


def _bias_kernel(rb_ref, bkt_ref, o_ref, *, head_patterns):
    for head, pat in enumerate(head_patterns):
        bkt = bkt_ref[pat]
        acc = jnp.zeros(bkt.shape, F32)
        for bucket in range(REL_BUCKETS):
            acc = jnp.where(bkt == bucket, rb_ref[bucket, head], acc)
        o_ref[head] = acc


def _t5_bucket(dist):
    max_exact = REL_BUCKETS // 2
    d = jnp.maximum(dist, 1).astype(F32)
    large = max_exact + (jnp.log(d / max_exact) / math.log(REL_MAX_DIST / max_exact)
                         * (REL_BUCKETS - max_exact)).astype(jnp.int32)
    large = jnp.minimum(large, REL_BUCKETS - 1)
    return jnp.where(dist < max_exact, dist, large)


def _bias_tables(rel_bias):
    dist = (BLOCK + jnp.arange(BLOCK))[:, None] - jnp.arange(2 * BLOCK)[None, :]
    dist = jnp.maximum(dist, 0)
    buckets = jnp.stack([_t5_bucket(dist * dil) for _, dil in DIL_PATTERNS]).astype(jnp.int32)
    head_patterns = (0,) * SW_Q_HEADS + tuple(g for g in range(N_DIL) for _ in range(DIL_HEADS))
    n_heads = len(head_patterns)
    return pl.pallas_call(
        functools.partial(_bias_kernel, head_patterns=head_patterns),
        in_specs=[pl.BlockSpec(memory_space=pltpu.SMEM), _full(buckets.shape)],
        out_specs=_full((n_heads, BLOCK, 2 * BLOCK)),
        out_shape=jax.ShapeDtypeStruct((n_heads, BLOCK, 2 * BLOCK), F32),
        grid=(1,),
        name="rel_bias",
    )(rel_bias, buckets)


def _merge_kernel(*refs, with_router):
    (x_ref, glow_ref, oa_ref, ob_ref, oc_ref, d0_ref, d1_ref, d2_ref, l0_ref, l1_ref, l2_ref,
     wgb_ref, bg_ref, wbr_ref, wout_ref, g2_ref) = refs[:16]
    rest = refs[16:]
    if with_router:
        wr_hi_ref, wr_lo_ref, br_ref = rest[:3]
        rest = rest[3:]
    x1_ref, h2_ref = rest[:2]

    l0, l1, l2 = l0_ref[...], l1_ref[...], l2_ref[...]
    mx = jnp.maximum(jnp.maximum(l0, l1), l2)
    e0, e1, e2 = jnp.exp(l0 - mx), jnp.exp(l1 - mx), jnp.exp(l2 - mx)
    od = (e0 * d0_ref[...] + e1 * d1_ref[...] + e2 * d2_ref[...]) / (e0 + e1 + e2)

    glow = glow_ref[...]
    branches = (oa_ref[...], ob_ref[...], oc_ref[...], od.astype(BF16))
    y = None
    row = 0
    for i, o in enumerate(branches):
        cols = slice(i * D_MODEL, (i + 1) * D_MODEL)
        pre = _dot(glow, wgb_ref[:, cols]) + bg_ref[:, cols]
        gate = 1.0 / (1.0 + jnp.exp(-pre))
        term = gate * _dot(o, wbr_ref[row:row + BR_SIZES[i], :])
        y = term if y is None else y + term
        row += BR_SIZES[i]
    x1 = x_ref[...] + _dot(y.astype(BF16), wout_ref[...])
    x1_ref[...] = x1
    h2 = _rms(x1, g2_ref[...])
    h2_ref[...] = h2.astype(BF16)

    if with_router:
        wts_ref = rest[2]
        hi, lo = _split_bf16(h2)
        whi = wr_hi_ref[...]
        lg = _dot(hi, whi) + _dot(lo, whi) + _dot(hi, wr_lo_ref[...]) + br_ref[...]
        lane = lax.broadcasted_iota(jnp.int32, lg.shape, 1).astype(F32)
        big = float(LANES)
        m1 = jnp.max(lg, axis=-1, keepdims=True)
        i1 = jnp.min(jnp.where(lg == m1, lane, big), axis=-1, keepdims=True)
        lg2 = jnp.where(lane == i1, NEG, lg)
        m2 = jnp.max(lg2, axis=-1, keepdims=True)
        i2 = jnp.min(jnp.where(lg2 == m2, lane, big), axis=-1, keepdims=True)
        e = jnp.exp(m2 - m1)
        w1 = 1.0 / (1.0 + e)
        w2 = e / (1.0 + e)
        wts_ref[...] = jnp.where(lane == i1, w1, 0.0) + jnp.where(lane == i2, w2, 0.0)


def _merge(x2, glow, oa, ob, oc, dil_o, dil_l, wgb, bg, wbr, wout, g2, router):
    n = x2.shape[0]
    tm = ROW_TILE
    row = lambda w: pl.BlockSpec((tm, w), lambda i: (i, 0))
    acts = [x2, glow, oa, ob, oc, *dil_o, *dil_l]
    weights = [wgb, bg, wbr, wout, g2] + (list(router) if router is not None else [])
    out_shape = [jax.ShapeDtypeStruct((n, D_MODEL), F32), jax.ShapeDtypeStruct((n, D_MODEL), BF16)]
    out_specs = [row(D_MODEL), row(D_MODEL)]
    if router is not None:
        out_shape.append(jax.ShapeDtypeStruct((n, LANES), F32))
        out_specs.append(row(LANES))
    return pl.pallas_call(
        functools.partial(_merge_kernel, with_router=router is not None),
        grid=(n // tm,),
        in_specs=[row(a.shape[1]) for a in acts] + [_full(w.shape) for w in weights],
        out_specs=out_specs,
        out_shape=out_shape,
        compiler_params=_params(("parallel",)),
        name="merge",
    )(*acts, *weights)


FF_CHUNK = 512


def _dense_ffn_kernel(x1_ref, h2_ref, wg_ref, wu_ref, wd_ref, o_ref):
    h2 = h2_ref[...]
    acc = x1_ref[...]
    for c in range(D_FF_DENSE // FF_CHUNK):
        cols = slice(c * FF_CHUNK, (c + 1) * FF_CHUNK)
        g = _dot(h2, wg_ref[:, cols])
        u = _dot(h2, wu_ref[:, cols])
        act = (g / (1.0 + jnp.exp(-g)) * u).astype(BF16)
        acc = acc + _dot(act, wd_ref[cols, :])
    o_ref[...] = acc


def _dense_ffn(x1, h2, wg, wu, wd):
    n = x1.shape[0]
    tm = ROW_TILE
    row = pl.BlockSpec((tm, D_MODEL), lambda i: (i, 0))
    return pl.pallas_call(
        _dense_ffn_kernel,
        grid=(n // tm,),
        in_specs=[row, row, _full(wg.shape), _full(wu.shape), _full(wd.shape)],
        out_specs=row,
        out_shape=jax.ShapeDtypeStruct((n, D_MODEL), F32),
        compiler_params=_params(("parallel",)),
        name="dense_ffn",
    )(x1, h2, wg, wu, wd)


def _moe_kernel(x1_ref, h2_ref, wts_ref, wg_ref, wu_ref, wd_ref, o_ref, acc_ref):
    e = pl.program_id(1)

    @pl.when(e == 0)
    def _():
        acc_ref[...] = x1_ref[...]

    wts = wts_ref[...]
    lane = lax.broadcasted_iota(jnp.int32, wts.shape, 1)
    w_e = jnp.sum(jnp.where(lane == e, wts, 0.0), axis=-1, keepdims=True)
    h2 = h2_ref[...]
    g = _dot(h2, wg_ref[0])
    u = _dot(h2, wu_ref[0])
    act = (g / (1.0 + jnp.exp(-g)) * u).astype(BF16)
    acc_ref[...] += w_e * _dot(act, wd_ref[0])

    @pl.when(e == N_EXPERTS - 1)
    def _():
        o_ref[...] = acc_ref[...]


def _moe_ffn(x1, h2, wts, wg, wu, wd):
    n = x1.shape[0]
    tm = ROW_TILE
    row = lambda w: pl.BlockSpec((tm, w), lambda i, e: (i, 0))
    return pl.pallas_call(
        _moe_kernel,
        grid=(n // tm, N_EXPERTS),
        in_specs=[row(D_MODEL), row(D_MODEL), row(LANES),
                  pl.BlockSpec((1, D_MODEL, D_FF_EXPERT), lambda i, e: (e, 0, 0)),
                  pl.BlockSpec((1, D_MODEL, D_FF_EXPERT), lambda i, e: (e, 0, 0)),
                  pl.BlockSpec((1, D_FF_EXPERT, D_MODEL), lambda i, e: (e, 0, 0))],
        out_specs=row(D_MODEL),
        out_shape=jax.ShapeDtypeStruct((n, D_MODEL), F32),
        scratch_shapes=[pltpu.VMEM((tm, D_MODEL), F32)],
        compiler_params=_params(("parallel", "arbitrary")),
        name="moe_ffn",
    )(x1, h2, wts, wg, wu, wd)


def _rope_tables(seq):
    half = MLA_ROPE // 2
    inv = ROPE_THETA ** (-jnp.arange(half, dtype=F32) / half)
    ang = jnp.arange(seq, dtype=F32)[:, None] * inv[None, :]
    cos, sin = jnp.cos(ang), jnp.sin(ang)
    zeros = lambda w: jnp.zeros((seq, w), F32)
    tail = LANES - MLA_QK
    rc = jnp.concatenate([jnp.ones((seq, MLA_NOPE), F32), cos, cos, zeros(tail)], axis=1)
    rs1 = jnp.concatenate([zeros(MLA_NOPE + half), sin, zeros(tail)], axis=1)
    rs2 = jnp.concatenate([zeros(MLA_NOPE), -sin, zeros(half + tail)], axis=1)
    return rc, rs1, rs2


def _pad_heads(w, heads, dim):
    rows = w.shape[0]
    w = w.reshape(rows, heads, dim)
    return jnp.pad(w, ((0, 0), (0, 0), (0, LANES - dim))).reshape(rows, heads * LANES)


def _layer_weights(w_in, g_qa, w_qb, g_kva, w_kvb, qk_g_mla, qk_g_sw, qk_g_dil, w_gate_a):
    offs = np.concatenate([[0], np.cumsum(IN_SIZES)]).tolist()
    cols = [w_in[:, offs[j]:offs[j + 1]] for j in range(len(IN_SIZES))]
    a_q, a_k, a_v, b_cq, b_ckv, b_kpe, c_q, c_k, c_v, d_q, d_k, d_v = cols
    wa = jnp.concatenate([a_q * (SB_DIM ** -0.5), a_k, a_v], axis=1)
    kpe_pad = jnp.pad(b_kpe, ((0, 0), (MLA_NOPE, LANES - MLA_QK)))
    wb = jnp.concatenate([b_cq, b_ckv] + [kpe_pad] * MLA_HEADS, axis=1)
    rep = SW_Q_HEADS // SW_KV_HEADS
    expand = lambda w: jnp.repeat(w.reshape(D_MODEL, SW_KV_HEADS, SW_DIM), rep, axis=1).reshape(D_MODEL, SW_W)
    wc = jnp.concatenate([c_q, expand(c_k), expand(c_v)], axis=1)
    wd = jnp.concatenate([d_q, d_k, d_v], axis=1)
    wqb = _pad_heads(w_qb, MLA_HEADS, MLA_QK)
    kvb = w_kvb.reshape(MLA_KV_RANK, MLA_HEADS, MLA_NOPE + MLA_V)
    wkvk = _pad_heads(kvb[:, :, :MLA_NOPE].reshape(MLA_KV_RANK, -1), MLA_HEADS, MLA_NOPE)
    wkvv = kvb[:, :, MLA_NOPE:].reshape(MLA_KV_RANK, MLA_HEADS * MLA_V)
    pad_gain = lambda g: jnp.tile(jnp.pad(g, (0, LANES - MLA_QK)), MLA_HEADS)[None, :]
    gmq = pad_gain(qk_g_mla[0]) * (MLA_QK ** -0.5 * math.log2(math.e))
    gmk = pad_gain(qk_g_mla[1])
    gsq = jnp.tile(qk_g_sw[0], SW_Q_HEADS)[None, :] * (SW_DIM ** -0.5)
    gsk = jnp.tile(qk_g_sw[1], SW_Q_HEADS)[None, :]
    gdq = jnp.tile(qk_g_dil[0], N_DIL * DIL_HEADS)[None, :] * (DIL_DIM ** -0.5)
    gdk = jnp.tile(qk_g_dil[1], N_DIL * DIL_HEADS)[None, :]
    bf = lambda w: w.astype(BF16)
    return (bf(wa), bf(wb), bf(wc), bf(wd), bf(w_gate_a), g_qa[None, :], bf(wqb), g_kva[None, :],
            bf(wkvk), bf(wkvv)), (gmq, gmk, gsq, gsk, gdq, gdk)


def kernel(x, norm1_g, w_in, g_qa, w_qb, g_kva, w_kvb, qk_g_mla, qk_g_sw, qk_g_dil, sinks, rel_bias, w_gate_a, w_gate_b, b_gate, w_branch, w_out, norm2_g, w_gu_dense, w_down_dense, w_router, b_router, w_gu_exp, w_down_exp):
    b, s, d = x.shape
    n = b * s
    depth = norm1_g.shape[0]
    bias = _bias_tables(rel_bias)
    sw_bias = bias[:SW_Q_HEADS]
    dil_bias = [bias[SW_Q_HEADS + g * DIL_HEADS:SW_Q_HEADS + (g + 1) * DIL_HEADS] for g in range(N_DIL)]
    rc, rs1, rs2 = _rope_tables(s)
    x2 = x.reshape(n, d)
    seq3 = lambda a: a.reshape(b, s, a.shape[-1])
    flat = lambda a: a.reshape(n, a.shape[-1])

    for i in range(depth):
        weights, gains = _layer_weights(w_in[i], g_qa[i], w_qb[i], g_kva[i], w_kvb[i], qk_g_mla[i],
                                        qk_g_sw[i], qk_g_dil[i], w_gate_a[i])
        wa, wb, wc, wd, wg, gqa, wqb, gkva, wkvk, wkvv = weights
        (sbq, sbk, sbv, mq, mk, mv, swq, swk, swv,
         dq0, dq1, dq2, dk0, dk1, dk2, dv0, dv1, dv2, glow) = _inproj(
            x2, norm1_g[i][None, :], wa, wb, wc, wd, wg, gqa, wqb, gkva, wkvk, wkvv, rc, rs1, rs2,
            *gains, s)

        out_a = _sb_attention(seq3(sbq), seq3(sbk), seq3(sbv))
        out_b = _mla_attention(seq3(mq), seq3(mk), seq3(mv))
        (out_c,) = _band_attention(seq3(swq), seq3(swk), seq3(swv), sw_bias, sinks[i], 1,
                                   SW_Q_HEADS, False, False, BF16)
        dil_o, dil_l = [], []
        for g, (dq, dk, dv) in enumerate(((dq0, dk0, dv0), (dq1, dk1, dv1), (dq2, dk2, dv2))):
            o, l = _band_attention(seq3(dq), seq3(dk), seq3(dv), dil_bias[g], None,
                                   DIL_PATTERNS[g][1], DIL_HEADS, True, True, F32)
            dil_o.append(flat(o))
            dil_l.append(flat(l))

        router = None
        if i % 2 == 1:
            wr = jnp.pad(w_router[i // 2], ((0, 0), (0, LANES - N_EXPERTS)))
            wr_hi = wr.astype(BF16)
            wr_lo = (wr - wr_hi.astype(F32)).astype(BF16)
            br = jnp.pad(b_router[i // 2], (0, LANES - N_EXPERTS), constant_values=NEG)[None, :]
            router = (wr_hi, wr_lo, br)
        merged = _merge(x2, glow, flat(out_a), flat(out_b), flat(out_c), dil_o, dil_l,
                        w_gate_b[i].astype(BF16), b_gate[i][None, :], w_branch[i].astype(BF16),
                        w_out[i].astype(BF16), norm2_g[i][None, :], router)
        if i % 2 == 0:
            x1, h2 = merged
            w_gu = w_gu_dense[i // 2].astype(BF16)
            x2 = _dense_ffn(x1, h2, w_gu[:, :D_FF_DENSE], w_gu[:, D_FF_DENSE:],
                            w_down_dense[i // 2].astype(BF16))
        else:
            x1, h2, wts = merged
            w_gu = w_gu_exp[i // 2].astype(BF16)
            x2 = _moe_ffn(x1, h2, wts, w_gu[:, :, :D_FF_EXPERT], w_gu[:, :, D_FF_EXPERT:],
                          w_down_exp[i // 2].astype(BF16))
    return x2.reshape(b, s, d)
```

```python
import functools
import math

import jax
import jax.numpy as jnp
import numpy as np
from jax import lax
from jax.experimental import pallas as pl
from jax.experimental.pallas import tpu as pltpu

F32 = jnp.float32
BF16 = jnp.bfloat16

D_MODEL = 1024
BLOCK = 128
EPS = 1e-6
SB_HEADS, SB_DIM = 4, 64
MLA_HEADS, MLA_Q_RANK, MLA_KV_RANK = 4, 256, 256
MLA_NOPE, MLA_ROPE, MLA_V = 64, 32, 64
MLA_QK = MLA_NOPE + MLA_ROPE
ROPE_THETA = 10000.0
SW_Q_HEADS, SW_KV_HEADS, SW_DIM = 8, 2, 32
DIL_PATTERNS = ((128, 1), (512, 4), (2048, 16))
DIL_HEADS, DIL_DIM = 4, 32
N_DIL = len(DIL_PATTERNS)
REL_BUCKETS, REL_MAX_DIST = 32, 2048
GATE_RANK, N_BRANCH = 128, 4
D_FF_DENSE = 2048
N_EXPERTS, TOP_K, D_FF_EXPERT = 8, 2, 768

SB_W = SB_HEADS * SB_DIM
SW_W = SW_Q_HEADS * SW_DIM
SW_KV_W = SW_KV_HEADS * SW_DIM
DIL_GW = DIL_HEADS * DIL_DIM
DIL_W = N_DIL * DIL_GW
IN_SIZES = (SB_W, SB_W, SB_W, MLA_Q_RANK, MLA_KV_RANK, MLA_ROPE, SW_W, SW_KV_W, SW_KV_W, DIL_W, DIL_W, DIL_W)
BR_SIZES = (SB_W, MLA_HEADS * MLA_V, SW_W, DIL_GW)

LANES = 128
MLA_PAD = LANES
NEG = -1e30
VMEM_LIMIT = 56 * 1024 * 1024

ROW_TILE = 512
ATT_TILE = 256
MLA_TILE = 512
SB_LOG_FLOOR = -105.0
BAND_BLOCKS = 4


def _dot(a, b):
    return jnp.dot(a, b, preferred_element_type=F32)


def _dot_nt(a, b):
    return lax.dot_general(a, b, (((1,), (1,)), ((), ())), preferred_element_type=F32)


def _dot_tn(a, b):
    return lax.dot_general(a, b, (((0,), (0,)), ((), ())), preferred_element_type=F32)


def _params(sem):
    return pltpu.CompilerParams(dimension_semantics=sem, vmem_limit_bytes=VMEM_LIMIT)


def _full(shape):
    nd = len(shape)
    return pl.BlockSpec(shape, lambda *_: (0,) * nd)


def _rms(v, g):
    return v * lax.rsqrt(jnp.mean(v * v, axis=-1, keepdims=True) + EPS) * g


def _split_bf16(v):
    hi = v.astype(BF16)
    lo = (v - hi.astype(F32)).astype(BF16)
    return hi, lo


def _group_matrix(group):
    r = lax.broadcasted_iota(jnp.int32, (LANES, LANES), 0) // group
    c = lax.broadcasted_iota(jnp.int32, (LANES, LANES), 1) // group
    return jnp.where(r == c, 1.0, 0.0).astype(BF16)


def _group_norm_slab(xs, gmat, inv_n):
    hi, lo = _split_bf16(xs * xs)
    ss = _dot(hi, gmat) + _dot(lo, gmat)
    return xs * lax.rsqrt(ss * inv_n + EPS)


def _inproj_kernel(x_ref, g1_ref, wa_ref, wb_ref, wc_ref, wd_ref, wg_ref,
                   gqa_ref, wqb_ref, gkva_ref, wkvk_ref, wkvv_ref,
                   rc_ref, rs1_ref, rs2_ref,
                   gmq_ref, gmk_ref, gsq_ref, gsk_ref, gdq_ref, gdk_ref,
                   sbq, sbk, sbv, mq, mk, mv, swq, swk, swv,
                   dq0, dq1, dq2, dk0, dk1, dk2, dv0, dv1, dv2, glow):
    x = x_ref[...]
    h = _rms(x, g1_ref[...]).astype(BF16)

    pa = _dot(h, wa_ref[...])
    sbq[...] = pa[:, :SB_W].astype(BF16)
    sbk[...] = pa[:, SB_W:2 * SB_W].astype(BF16)
    sbv[...] = pa[:, 2 * SB_W:].astype(BF16)

    pb = _dot(h, wb_ref[...])
    cq = pb[:, :MLA_Q_RANK]
    ckv = pb[:, MLA_Q_RANK:MLA_Q_RANK + MLA_KV_RANK]
    kpe = pb[:, MLA_Q_RANK + MLA_KV_RANK:]
    width = MLA_HEADS * MLA_PAD
    rc = jnp.concatenate([rc_ref[...]] * MLA_HEADS, axis=1)
    rs1 = jnp.concatenate([rs1_ref[...]] * MLA_HEADS, axis=1)
    rs2 = jnp.concatenate([rs2_ref[...]] * MLA_HEADS, axis=1)
    half = MLA_ROPE // 2

    def rope(v):
        return v * rc + pltpu.roll(v, half, 1) * rs1 + pltpu.roll(v, width - half, 1) * rs2

    ones = _group_matrix(LANES)
    q = rope(_dot(_rms(cq, gqa_ref[...]).astype(BF16), wqb_ref[...]))
    kvn = _rms(ckv, gkva_ref[...]).astype(BF16)
    k = _dot(kvn, wkvk_ref[...]) + rope(kpe)
    mv[...] = _dot(kvn, wkvv_ref[...]).astype(BF16)
    gmq = gmq_ref[...]
    gmk = gmk_ref[...]
    for s in range(MLA_HEADS):
        sl = slice(s * LANES, (s + 1) * LANES)
        mq[:, sl] = (_group_norm_slab(q[:, sl], ones, 1.0 / MLA_QK) * gmq[:, sl]).astype(BF16)
        mk[:, sl] = (_group_norm_slab(k[:, sl], ones, 1.0 / MLA_QK) * gmk[:, sl]).astype(BF16)

    g32 = _group_matrix(SW_DIM)
    pc = _dot(h, wc_ref[...])
    gsq = gsq_ref[...]
    gsk = gsk_ref[...]
    for s in range(SW_W // LANES):
        sl = slice(s * LANES, (s + 1) * LANES)
        ks = slice(SW_W + s * LANES, SW_W + (s + 1) * LANES)
        swq[:, sl] = (_group_norm_slab(pc[:, sl], g32, 1.0 / SW_DIM) * gsq[:, sl]).astype(BF16)
        swk[:, sl] = (_group_norm_slab(pc[:, ks], g32, 1.0 / SW_DIM) * gsk[:, sl]).astype(BF16)
    swv[...] = pc[:, 2 * SW_W:].astype(BF16)

    pd = _dot(h, wd_ref[...])
    gdq = gdq_ref[...]
    gdk = gdk_ref[...]
    for g, (oq, ok, ov) in enumerate(((dq0, dk0, dv0), (dq1, dk1, dv1), (dq2, dk2, dv2))):
        sl = slice(g * LANES, (g + 1) * LANES)
        ks = slice(DIL_W + g * LANES, DIL_W + (g + 1) * LANES)
        vs = slice(2 * DIL_W + g * LANES, 2 * DIL_W + (g + 1) * LANES)
        oq[...] = (_group_norm_slab(pd[:, sl], g32, 1.0 / DIL_DIM) * gdq[:, sl]).astype(BF16)
        ok[...] = (_group_norm_slab(pd[:, ks], g32, 1.0 / DIL_DIM) * gdk[:, sl]).astype(BF16)
        ov[...] = pd[:, vs].astype(BF16)

    glow[...] = _dot(h, wg_ref[...]).astype(BF16)


def _inproj(x2, g1, wa, wb, wc, wd, wg, gqa, wqb, gkva, wkvk, wkvv, rc, rs1, rs2,
            gmq, gmk, gsq, gsk, gdq, gdk, seq):
    n = x2.shape[0]
    tm = ROW_TILE
    n_seq_tiles = seq // tm
    row = lambda w: pl.BlockSpec((tm, w), lambda i: (i, 0))
    pos = lambda w: pl.BlockSpec((tm, w), lambda i: (i % n_seq_tiles, 0))
    weights = (g1, wa, wb, wc, wd, wg, gqa, wqb, gkva, wkvk, wkvv)
    gains = (gmq, gmk, gsq, gsk, gdq, gdk)
    out_widths = (SB_W,) * 3 + (MLA_HEADS * MLA_PAD,) * 2 + (MLA_HEADS * MLA_V,) + (SW_W,) * 3 \
        + (DIL_GW,) * 9 + (GATE_RANK,)
    return pl.pallas_call(
        _inproj_kernel,
        grid=(n // tm,),
        in_specs=[row(D_MODEL)] + [_full(w.shape) for w in weights] + [pos(LANES)] * 3
        + [_full(g.shape) for g in gains],
        out_specs=[row(w) for w in out_widths],
        out_shape=[jax.ShapeDtypeStruct((n, w), BF16) for w in out_widths],
        compiler_params=_params(("parallel",)),
        name="inproj",
    )(x2, *weights, rc, rs1, rs2, *gains)


def _sb_kernel(q_ref, k_ref, v_ref, o_ref, acc_ref, car_ref):
    t = ATT_TILE
    i = pl.program_id(2)
    q = q_ref[0]
    lane = lax.broadcasted_iota(jnp.int32, (t, LANES), 1)
    zero = jnp.zeros_like(q)
    qh = (jnp.where(lane < SB_DIM, q, zero), jnp.where(lane >= SB_DIM, q, zero))
    r = lax.broadcasted_iota(jnp.int32, (t, t), 0)
    c = lax.broadcasted_iota(jnp.int32, (t, t), 1)
    suffix = jnp.where(r >= c, 1.0, 0.0).astype(BF16)
    strict = c < r

    acc_ref[...] = jnp.zeros_like(acc_ref)
    car_ref[...] = jnp.zeros_like(car_ref)

    def block(j, diagonal):
        off = pl.multiple_of(j * t, t)
        kb = k_ref[0, pl.ds(off, t), :]
        vb = v_ref[0, pl.ds(off, t), :]
        for h in range(2):
            z = _dot_nt(qh[h], kb)
            lk = -(jnp.maximum(z, 0.0) + jnp.log(1.0 + jnp.exp(-jnp.abs(z))))
            if diagonal:
                lk = jnp.where(strict, lk, 0.0)
            within = _dot(lk.astype(BF16), suffix)
            car = car_ref[h]
            a = jnp.exp(jnp.minimum(z + within, 0.0) + car)
            if diagonal:
                a = jnp.where(strict, a, 0.0)
            acc_ref[h] += _dot(a.astype(BF16), vb)
            car_ref[h] = car + within[:, :1]

    block(i, True)

    def live(s):
        return jnp.logical_and(s < i, jnp.max(car_ref[...]) > SB_LOG_FLOOR)

    def body(s):
        block(i - 1 - s, False)
        return s + 1

    lax.while_loop(live, body, 0)
    o_ref[0] = jnp.where(lane < SB_DIM, acc_ref[0], acc_ref[1]).astype(o_ref.dtype)


def _sb_attention(q, k, v):
    b, s, _ = q.shape
    t = ATT_TILE
    return pl.pallas_call(
        _sb_kernel,
        grid=(b, SB_W // LANES, s // t),
        in_specs=[pl.BlockSpec((1, t, LANES), lambda bi, p, i: (bi, i, p)),
                  pl.BlockSpec((1, s, LANES), lambda bi, p, i: (bi, 0, p)),
                  pl.BlockSpec((1, s, LANES), lambda bi, p, i: (bi, 0, p))],
        out_specs=pl.BlockSpec((1, t, LANES), lambda bi, p, i: (bi, i, p)),
        out_shape=jax.ShapeDtypeStruct((b, s, SB_W), BF16),
        scratch_shapes=[pltpu.VMEM((2, t, LANES), F32), pltpu.VMEM((2, t, 1), F32)],
        compiler_params=_params(("parallel", "parallel", "arbitrary")),
        name="stick_breaking",
    )(q, k, v)


def _mla_kernel(q_ref, k_ref, v_ref, o_ref, acc_ref, m_ref, l_ref):
    t = MLA_TILE
    i = pl.program_id(2)
    q = q_ref[0]
    r = lax.broadcasted_iota(jnp.int32, (t, t), 0)
    c = lax.broadcasted_iota(jnp.int32, (t, t), 1)
    causal = c <= r
    lane = lax.broadcasted_iota(jnp.int32, (t, LANES), 1)
    reps = t // LANES

    acc_ref[...] = jnp.zeros_like(acc_ref)
    l_ref[...] = jnp.zeros_like(l_ref)
    m_ref[...] = jnp.full_like(m_ref, NEG)

    def block(j, diagonal):
        off = pl.multiple_of(j * t, t)
        kb = k_ref[0, pl.ds(off, t), :]
        vb = v_ref[0, pl.ds(off, t), :]
        for h in range(2):
            sl = slice(h * MLA_PAD, (h + 1) * MLA_PAD)
            s = _dot_nt(q[:, sl], kb[:, sl])
            if diagonal:
                s = jnp.where(causal, s, NEG)
            m_old = m_ref[h]
            m_new = jnp.maximum(m_old, jnp.max(s, axis=-1, keepdims=True))
            alpha = jnp.exp2(m_old - m_new)
            p = jnp.exp2(s - jnp.tile(m_new, (1, reps)))
            l_ref[h] = alpha * l_ref[h] + jnp.sum(p, axis=-1, keepdims=True)
            acc_ref[h] = alpha * acc_ref[h] + _dot(p.astype(BF16), vb)
            m_ref[h] = m_new

    block(i, True)

    def body(s, carry):
        block(i - 1 - s, False)
        return carry

    lax.fori_loop(0, i, body, 0)
    o_ref[0] = jnp.where(lane < MLA_V, acc_ref[0] / l_ref[0], acc_ref[1] / l_ref[1]).astype(o_ref.dtype)


def _mla_attention(q, k, v):
    b, s, _ = q.shape
    t = MLA_TILE
    stat = pltpu.VMEM((2, t, LANES), F32)
    return pl.pallas_call(
        _mla_kernel,
        grid=(b, MLA_HEADS // 2, s // t),
        in_specs=[pl.BlockSpec((1, t, 2 * MLA_PAD), lambda bi, p, i: (bi, i, p)),
                  pl.BlockSpec((1, s, 2 * MLA_PAD), lambda bi, p, i: (bi, 0, p)),
                  pl.BlockSpec((1, s, LANES), lambda bi, p, i: (bi, 0, p))],
        out_specs=pl.BlockSpec((1, t, LANES), lambda bi, p, i: (bi, i, p)),
        out_shape=jax.ShapeDtypeStruct((b, s, MLA_HEADS * MLA_V), BF16),
        scratch_shapes=[stat, stat, stat],
        compiler_params=_params(("parallel", "parallel", "arbitrary")),
        name="latent_attention",
    )(q, k, v)


def _band_kernel(*refs, heads, inclusive, with_sinks, with_lse):
    q_ref, kp_ref, kc_ref, vp_ref, vc_ref, bias_ref = refs[:6]
    rest = refs[6:]
    if with_sinks:
        sink_ref, rest = rest[0], rest[1:]
    o_ref = rest[0]
    width = heads * DIL_DIM
    cols = heads * BLOCK
    n = pl.program_id(2)
    k_all = jnp.concatenate([kp_ref[0], kc_ref[0]], axis=0)
    v_all = jnp.concatenate([vp_ref[0], vc_ref[0]], axis=0)
    kj = lax.broadcasted_iota(jnp.int32, (2 * BLOCK, cols), 0)
    qi = lax.broadcasted_iota(jnp.int32, (2 * BLOCK, cols), 1) & (BLOCK - 1)
    near = (kj >= qi) if inclusive else (kj > qi)
    in_window = near & (kj <= qi + BLOCK)
    first_mask = in_window & ((kj >= BLOCK) | (n > 0))
    group = lax.broadcasted_iota(jnp.int32, (BLOCK, width), 1) // DIL_DIM
    bias = bias_ref[...]
    for j in range(BAND_BLOCKS):
        q = q_ref[0, j * BLOCK:(j + 1) * BLOCK, :]
        zero = jnp.zeros_like(q)
        qs = jnp.concatenate([jnp.where(group == h, q, zero) for h in range(heads)], axis=0)
        k = k_all[j * BLOCK:(j + 2) * BLOCK]
        v = v_all[j * BLOCK:(j + 2) * BLOCK]
        s = _dot_nt(k, qs) + bias
        s = jnp.where(first_mask if j == 0 else in_window, s, NEG)
        m = jnp.max(s, axis=0, keepdims=True)
        if with_sinks:
            sink = sink_ref[...]
            m = jnp.maximum(m, sink)
        p = jnp.exp(s - m)
        l = jnp.sum(p, axis=0, keepdims=True)
        if with_sinks:
            l = l + jnp.exp(sink - m)
        o = _dot_tn(v, p.astype(BF16)) * (1.0 / l)
        out = jnp.concatenate([o[h * DIL_DIM:(h + 1) * DIL_DIM, h * BLOCK:(h + 1) * BLOCK]
                               for h in range(heads)], axis=0)
        o_ref[0, j * BLOCK:(j + 1) * BLOCK, :] = out.T.astype(o_ref.dtype)
        if with_lse:
            ls = m + jnp.log(l)
            lse = jnp.concatenate([jnp.broadcast_to(ls[:, h * BLOCK:(h + 1) * BLOCK], (DIL_DIM, BLOCK))
                                   for h in range(heads)], axis=0)
            rest[1][0, j * BLOCK:(j + 1) * BLOCK, :] = lse.T


def _band_attention(q, k, v, bias, sinks, dil, heads, inclusive, with_lse, out_dtype):
    b, s, width = q.shape
    m = s // dil
    step = BAND_BLOCKS * BLOCK
    nb = m // step
    view = lambda a: a.reshape(b, m, dil * width)
    cur = pl.BlockSpec((1, step, width), lambda bi, r, n: (bi, n, r))
    prev = pl.BlockSpec((1, BLOCK, width), lambda bi, r, n: (bi, jnp.maximum(n * BAND_BLOCKS - 1, 0), r))
    bias = bias.transpose(2, 0, 1).reshape(2 * BLOCK, heads * BLOCK)
    in_specs = [cur, prev, cur, prev, cur, _full(bias.shape)]
    args = [view(q), view(k), view(k), view(v), view(v), bias]
    if sinks is not None:
        sink_cols = jnp.repeat(sinks, BLOCK)[None, :]
        in_specs.append(_full(sink_cols.shape))
        args.append(sink_cols)
    out_shape = [jax.ShapeDtypeStruct((b, m, dil * width), out_dtype)]
    out_specs = [cur]
    if with_lse:
        out_shape.append(jax.ShapeDtypeStruct((b, m, dil * width), F32))
        out_specs.append(cur)
    outs = pl.pallas_call(
        functools.partial(_band_kernel, heads=heads, inclusive=inclusive,
                          with_sinks=sinks is not None, with_lse=with_lse),
        grid=(b, dil, nb),
        in_specs=in_specs,
        out_specs=out_specs,
        out_shape=out_shape,
        compiler_params=_params(("parallel", "parallel", "arbitrary")),
        name="band_attention",
    )(*args)
    return [o.reshape(b, s, width) for o in outs]


def _bias_kernel(rb_ref, bkt_ref, o_ref, *, head_patterns):
    for head, pat in enumerate(head_patterns):
        bkt = bkt_ref[pat]
        acc = jnp.zeros(bkt.shape, F32)
        for bucket in range(REL_BUCKETS):
            acc = jnp.where(bkt == bucket, rb_ref[bucket, head], acc)
        o_ref[head] = acc


def _t5_bucket(dist):
    max_exact = REL_BUCKETS // 2
    d = jnp.maximum(dist, 1).astype(F32)
    large = max_exact + (jnp.log(d / max_exact) / math.log(REL_MAX_DIST / max_exact)
                         * (REL_BUCKETS - max_exact)).astype(jnp.int32)
    large = jnp.minimum(large, REL_BUCKETS - 1)
    return jnp.where(dist < max_exact, dist, large)


def _bias_tables(rel_bias):
    dist = (BLOCK + jnp.arange(BLOCK))[:, None] - jnp.arange(2 * BLOCK)[None, :]
    dist = jnp.maximum(dist, 0)
    buckets = jnp.stack([_t5_bucket(dist * dil) for _, dil in DIL_PATTERNS]).astype(jnp.int32)
    head_patterns = (0,) * SW_Q_HEADS + tuple(g for g in range(N_DIL) for _ in range(DIL_HEADS))
    n_heads = len(head_patterns)
    return pl.pallas_call(
        functools.partial(_bias_kernel, head_patterns=head_patterns),
        in_specs=[pl.BlockSpec(memory_space=pltpu.SMEM), _full(buckets.shape)],
        out_specs=_full((n_heads, BLOCK, 2 * BLOCK)),
        out_shape=jax.ShapeDtypeStruct((n_heads, BLOCK, 2 * BLOCK), F32),
        grid=(1,),
        name="rel_bias",
    )(rel_bias, buckets)


def _merge_kernel(*refs, with_router):
    (x_ref, glow_ref, oa_ref, ob_ref, oc_ref, d0_ref, d1_ref, d2_ref, l0_ref, l1_ref, l2_ref,
     wgb_ref, bg_ref, wbr_ref, wout_ref, g2_ref) = refs[:16]
    rest = refs[16:]
    if with_router:
        wr_hi_ref, wr_lo_ref, br_ref = rest[:3]
        rest = rest[3:]
    x1_ref, h2_ref = rest[:2]

    l0, l1, l2 = l0_ref[...], l1_ref[...], l2_ref[...]
    mx = jnp.maximum(jnp.maximum(l0, l1), l2)
    e0, e1, e2 = jnp.exp(l0 - mx), jnp.exp(l1 - mx), jnp.exp(l2 - mx)
    od = (e0 * d0_ref[...] + e1 * d1_ref[...] + e2 * d2_ref[...]) / (e0 + e1 + e2)

    glow = glow_ref[...]
    branches = (oa_ref[...], ob_ref[...], oc_ref[...], od.astype(BF16))
    y = None
    row = 0
    for i, o in enumerate(branches):
        cols = slice(i * D_MODEL, (i + 1) * D_MODEL)
        pre = _dot(glow, wgb_ref[:, cols]) + bg_ref[:, cols]
        gate = 1.0 / (1.0 + jnp.exp(-pre))
        term = gate * _dot(o, wbr_ref[row:row + BR_SIZES[i], :])
        y = term if y is None else y + term
        row += BR_SIZES[i]
    x1 = x_ref[...] + _dot(y.astype(BF16), wout_ref[...])
    x1_ref[...] = x1
    h2 = _rms(x1, g2_ref[...])
    h2_ref[...] = h2.astype(BF16)

    if with_router:
        wts_ref = rest[2]
        hi, lo = _split_bf16(h2)
        whi = wr_hi_ref[...]
        lg = _dot(hi, whi) + _dot(lo, whi) + _dot(hi, wr_lo_ref[...]) + br_ref[...]
        lane = lax.broadcasted_iota(jnp.int32, lg.shape, 1).astype(F32)
        big = float(LANES)
        m1 = jnp.max(lg, axis=-1, keepdims=True)
        i1 = jnp.min(jnp.where(lg == m1, lane, big), axis=-1, keepdims=True)
        lg2 = jnp.where(lane == i1, NEG, lg)
        m2 = jnp.max(lg2, axis=-1, keepdims=True)
        i2 = jnp.min(jnp.where(lg2 == m2, lane, big), axis=-1, keepdims=True)
        e = jnp.exp(m2 - m1)
        w1 = 1.0 / (1.0 + e)
        w2 = e / (1.0 + e)
        wts_ref[...] = jnp.where(lane == i1, w1, 0.0) + jnp.where(lane == i2, w2, 0.0)


def _merge(x2, glow, oa, ob, oc, dil_o, dil_l, wgb, bg, wbr, wout, g2, router):
    n = x2.shape[0]
    tm = ROW_TILE
    row = lambda w: pl.BlockSpec((tm, w), lambda i: (i, 0))
    acts = [x2, glow, oa, ob, oc, *dil_o, *dil_l]
    weights = [wgb, bg, wbr, wout, g2] + (list(router) if router is not None else [])
    out_shape = [jax.ShapeDtypeStruct((n, D_MODEL), F32), jax.ShapeDtypeStruct((n, D_MODEL), BF16)]
    out_specs = [row(D_MODEL), row(D_MODEL)]
    if router is not None:
        out_shape.append(jax.ShapeDtypeStruct((n, LANES), F32))
        out_specs.append(row(LANES))
    return pl.pallas_call(
        functools.partial(_merge_kernel, with_router=router is not None),
        grid=(n // tm,),
        in_specs=[row(a.shape[1]) for a in acts] + [_full(w.shape) for w in weights],
        out_specs=out_specs,
        out_shape=out_shape,
        compiler_params=_params(("parallel",)),
        name="merge",
    )(*acts, *weights)


FF_CHUNK = 512


def _dense_ffn_kernel(x1_ref, h2_ref, wg_ref, wu_ref, wd_ref, o_ref):
    h2 = h2_ref[...]
    acc = x1_ref[...]
    for c in range(D_FF_DENSE // FF_CHUNK):
        cols = slice(c * FF_CHUNK, (c + 1) * FF_CHUNK)
        g = _dot(h2, wg_ref[:, cols])
        u = _dot(h2, wu_ref[:, cols])
        act = (g / (1.0 + jnp.exp(-g)) * u).astype(BF16)
        acc = acc + _dot(act, wd_ref[cols, :])
    o_ref[...] = acc


def _dense_ffn(x1, h2, wg, wu, wd):
    n = x1.shape[0]
    tm = ROW_TILE
    row = pl.BlockSpec((tm, D_MODEL), lambda i: (i, 0))
    return pl.pallas_call(
        _dense_ffn_kernel,
        grid=(n // tm,),
        in_specs=[row, row, _full(wg.shape), _full(wu.shape), _full(wd.shape)],
        out_specs=row,
        out_shape=jax.ShapeDtypeStruct((n, D_MODEL), F32),
        compiler_params=_params(("parallel",)),
        name="dense_ffn",
    )(x1, h2, wg, wu, wd)


def _moe_kernel(x1_ref, h2_ref, wts_ref, wg_ref, wu_ref, wd_ref, o_ref, acc_ref):
    e = pl.program_id(1)

    @pl.when(e == 0)
    def _():
        acc_ref[...] = x1_ref[...]

    wts = wts_ref[...]
    lane = lax.broadcasted_iota(jnp.int32, wts.shape, 1)
    w_e = jnp.sum(jnp.where(lane == e, wts, 0.0), axis=-1, keepdims=True)
    h2 = h2_ref[...]
    g = _dot(h2, wg_ref[0])
    u = _dot(h2, wu_ref[0])
    act = (g / (1.0 + jnp.exp(-g)) * u).astype(BF16)
    acc_ref[...] += w_e * _dot(act, wd_ref[0])

    @pl.when(e == N_EXPERTS - 1)
    def _():
        o_ref[...] = acc_ref[...]


def _moe_ffn(x1, h2, wts, wg, wu, wd):
    n = x1.shape[0]
    tm = ROW_TILE
    row = lambda w: pl.BlockSpec((tm, w), lambda i, e: (i, 0))
    return pl.pallas_call(
        _moe_kernel,
        grid=(n // tm, N_EXPERTS),
        in_specs=[row(D_MODEL), row(D_MODEL), row(LANES),
                  pl.BlockSpec((1, D_MODEL, D_FF_EXPERT), lambda i, e: (e, 0, 0)),
                  pl.BlockSpec((1, D_MODEL, D_FF_EXPERT), lambda i, e: (e, 0, 0)),
                  pl.BlockSpec((1, D_FF_EXPERT, D_MODEL), lambda i, e: (e, 0, 0))],
        out_specs=row(D_MODEL),
        out_shape=jax.ShapeDtypeStruct((n, D_MODEL), F32),
        scratch_shapes=[pltpu.VMEM((tm, D_MODEL), F32)],
        compiler_params=_params(("parallel", "arbitrary")),
        name="moe_ffn",
    )(x1, h2, wts, wg, wu, wd)


def _rope_tables(seq):
    half = MLA_ROPE // 2
    inv = ROPE_THETA ** (-jnp.arange(half, dtype=F32) / half)
    ang = jnp.arange(seq, dtype=F32)[:, None] * inv[None, :]
    cos, sin = jnp.cos(ang), jnp.sin(ang)
    zeros = lambda w: jnp.zeros((seq, w), F32)
    tail = LANES - MLA_QK
    rc = jnp.concatenate([jnp.ones((seq, MLA_NOPE), F32), cos, cos, zeros(tail)], axis=1)
    rs1 = jnp.concatenate([zeros(MLA_NOPE + half), sin, zeros(tail)], axis=1)
    rs2 = jnp.concatenate([zeros(MLA_NOPE), -sin, zeros(half + tail)], axis=1)
    return rc, rs1, rs2


def _pad_heads(w, heads, dim):
    rows = w.shape[0]
    w = w.reshape(rows, heads, dim)
    return jnp.pad(w, ((0, 0), (0, 0), (0, LANES - dim))).reshape(rows, heads * LANES)


def _layer_weights(w_in, g_qa, w_qb, g_kva, w_kvb, qk_g_mla, qk_g_sw, qk_g_dil, w_gate_a):
    offs = np.concatenate([[0], np.cumsum(IN_SIZES)]).tolist()
    cols = [w_in[:, offs[j]:offs[j + 1]] for j in range(len(IN_SIZES))]
    a_q, a_k, a_v, b_cq, b_ckv, b_kpe, c_q, c_k, c_v, d_q, d_k, d_v = cols
    wa = jnp.concatenate([a_q * (SB_DIM ** -0.5), a_k, a_v], axis=1)
    kpe_pad = jnp.pad(b_kpe, ((0, 0), (MLA_NOPE, LANES - MLA_QK)))
    wb = jnp.concatenate([b_cq, b_ckv] + [kpe_pad] * MLA_HEADS, axis=1)
    rep = SW_Q_HEADS // SW_KV_HEADS
    expand = lambda w: jnp.repeat(w.reshape(D_MODEL, SW_KV_HEADS, SW_DIM), rep, axis=1).reshape(D_MODEL, SW_W)
    wc = jnp.concatenate([c_q, expand(c_k), expand(c_v)], axis=1)
    wd = jnp.concatenate([d_q, d_k, d_v], axis=1)
    wqb = _pad_heads(w_qb, MLA_HEADS, MLA_QK)
    kvb = w_kvb.reshape(MLA_KV_RANK, MLA_HEADS, MLA_NOPE + MLA_V)
    wkvk = _pad_heads(kvb[:, :, :MLA_NOPE].reshape(MLA_KV_RANK, -1), MLA_HEADS, MLA_NOPE)
    wkvv = kvb[:, :, MLA_NOPE:].reshape(MLA_KV_RANK, MLA_HEADS * MLA_V)
    pad_gain = lambda g: jnp.tile(jnp.pad(g, (0, LANES - MLA_QK)), MLA_HEADS)[None, :]
    gmq = pad_gain(qk_g_mla[0]) * (MLA_QK ** -0.5 * math.log2(math.e))
    gmk = pad_gain(qk_g_mla[1])
    gsq = jnp.tile(qk_g_sw[0], SW_Q_HEADS)[None, :] * (SW_DIM ** -0.5)
    gsk = jnp.tile(qk_g_sw[1], SW_Q_HEADS)[None, :]
    gdq = jnp.tile(qk_g_dil[0], N_DIL * DIL_HEADS)[None, :] * (DIL_DIM ** -0.5)
    gdk = jnp.tile(qk_g_dil[1], N_DIL * DIL_HEADS)[None, :]
    bf = lambda w: w.astype(BF16)
    return (bf(wa), bf(wb), bf(wc), bf(wd), bf(w_gate_a), g_qa[None, :], bf(wqb), g_kva[None, :],
            bf(wkvk), bf(wkvv)), (gmq, gmk, gsq, gsk, gdq, gdk)


def kernel(x, norm1_g, w_in, g_qa, w_qb, g_kva, w_kvb, qk_g_mla, qk_g_sw, qk_g_dil, sinks, rel_bias, w_gate_a, w_gate_b, b_gate, w_branch, w_out, norm2_g, w_gu_dense, w_down_dense, w_router, b_router, w_gu_exp, w_down_exp):
    b, s, d = x.shape
    n = b * s
    depth = norm1_g.shape[0]
    bias = _bias_tables(rel_bias)
    sw_bias = bias[:SW_Q_HEADS]
    dil_bias = [bias[SW_Q_HEADS + g * DIL_HEADS:SW_Q_HEADS + (g + 1) * DIL_HEADS] for g in range(N_DIL)]
    rc, rs1, rs2 = _rope_tables(s)
    x2 = x.reshape(n, d)
    seq3 = lambda a: a.reshape(b, s, a.shape[-1])
    flat = lambda a: a.reshape(n, a.shape[-1])

    for i in range(depth):
        weights, gains = _layer_weights(w_in[i], g_qa[i], w_qb[i], g_kva[i], w_kvb[i], qk_g_mla[i],
                                        qk_g_sw[i], qk_g_dil[i], w_gate_a[i])
        wa, wb, wc, wd, wg, gqa, wqb, gkva, wkvk, wkvv = weights
        (sbq, sbk, sbv, mq, mk, mv, swq, swk, swv,
         dq0, dq1, dq2, dk0, dk1, dk2, dv0, dv1, dv2, glow) = _inproj(
            x2, norm1_g[i][None, :], wa, wb, wc, wd, wg, gqa, wqb, gkva, wkvk, wkvv, rc, rs1, rs2,
            *gains, s)

        out_a = _sb_attention(seq3(sbq), seq3(sbk), seq3(sbv))
        out_b = _mla_attention(seq3(mq), seq3(mk), seq3(mv))
        (out_c,) = _band_attention(seq3(swq), seq3(swk), seq3(swv), sw_bias, sinks[i], 1,
                                   SW_Q_HEADS, False, False, BF16)
        dil_o, dil_l = [], []
        for g, (dq, dk, dv) in enumerate(((dq0, dk0, dv0), (dq1, dk1, dv1), (dq2, dk2, dv2))):
            o, l = _band_attention(seq3(dq), seq3(dk), seq3(dv), dil_bias[g], None,
                                   DIL_PATTERNS[g][1], DIL_HEADS, True, True, F32)
            dil_o.append(flat(o))
            dil_l.append(flat(l))

        router = None
        if i % 2 == 1:
            wr = jnp.pad(w_router[i // 2], ((0, 0), (0, LANES - N_EXPERTS)))
            wr_hi = wr.astype(BF16)
            wr_lo = (wr - wr_hi.astype(F32)).astype(BF16)
            br = jnp.pad(b_router[i // 2], (0, LANES - N_EXPERTS), constant_values=NEG)[None, :]
            router = (wr_hi, wr_lo, br)
        merged = _merge(x2, glow, flat(out_a), flat(out_b), flat(out_c), dil_o, dil_l,
                        w_gate_b[i].astype(BF16), b_gate[i][None, :], w_branch[i].astype(BF16),
                        w_out[i].astype(BF16), norm2_g[i][None, :], router)
        if i % 2 == 0:
            x1, h2 = merged
            w_gu = w_gu_dense[i // 2].astype(BF16)
            x2 = _dense_ffn(x1, h2, w_gu[:, :D_FF_DENSE], w_gu[:, D_FF_DENSE:],
                            w_down_dense[i // 2].astype(BF16))
        else:
            x1, h2, wts = merged
            w_gu = w_gu_exp[i // 2].astype(BF16)
            x2 = _moe_ffn(x1, h2, wts, w_gu[:, :, :D_FF_EXPERT], w_gu[:, :, D_FF_EXPERT:],
                          w_down_exp[i // 2].astype(BF16))
    return x2.reshape(b, s, d)
```

```python
import functools
import math

import jax
import jax.numpy as jnp
import numpy as np
from jax import lax
from jax.experimental import pallas as pl
from jax.experimental.pallas import tpu as pltpu

F32 = jnp.float32
BF16 = jnp.bfloat16

D_MODEL = 1024
BLOCK = 128
EPS = 1e-6
SB_HEADS, SB_DIM = 4, 64
MLA_HEADS, MLA_Q_RANK, MLA_KV_RANK = 4, 256, 256
MLA_NOPE, MLA_ROPE, MLA_V = 64, 32, 64
MLA_QK = MLA_NOPE + MLA_ROPE
ROPE_THETA = 10000.0
SW_Q_HEADS, SW_KV_HEADS, SW_DIM = 8, 2, 32
DIL_PATTERNS = ((128, 1), (512, 4), (2048, 16))
DIL_HEADS, DIL_DIM = 4, 32
N_DIL = len(DIL_PATTERNS)
REL_BUCKETS, REL_MAX_DIST = 32, 2048
GATE_RANK, N_BRANCH = 128, 4
D_FF_DENSE = 2048
N_EXPERTS, TOP_K, D_FF_EXPERT = 8, 2, 768

SB_W = SB_HEADS * SB_DIM
SW_W = SW_Q_HEADS * SW_DIM
SW_KV_W = SW_KV_HEADS * SW_DIM
DIL_GW = DIL_HEADS * DIL_DIM
DIL_W = N_DIL * DIL_GW
IN_SIZES = (SB_W, SB_W, SB_W, MLA_Q_RANK, MLA_KV_RANK, MLA_ROPE, SW_W, SW_KV_W, SW_KV_W, DIL_W, DIL_W, DIL_W)
BR_SIZES = (SB_W, MLA_HEADS * MLA_V, SW_W, DIL_GW)

LANES = 128
MLA_PAD = LANES
NEG = -1e30
VMEM_LIMIT = 56 * 1024 * 1024

ROW_TILE = 512
ATT_TILE = 256
MLA_TILE = 512
SB_LOG_FLOOR = -105.0
MLA_BOUND_MARGIN = 1.03
MLA_MAX_SHIFT = 60.0
BAND_BLOCKS = 4


def _dot(a, b):
    return jnp.dot(a, b, preferred_element_type=F32)


def _dot_nt(a, b):
    return lax.dot_general(a, b, (((1,), (1,)), ((), ())), preferred_element_type=F32)


def _dot_tn(a, b):
    return lax.dot_general(a, b, (((0,), (0,)), ((), ())), preferred_element_type=F32)


def _params(sem):
    return pltpu.CompilerParams(dimension_semantics=sem, vmem_limit_bytes=VMEM_LIMIT)


def _full(shape):
    nd = len(shape)
    return pl.BlockSpec(shape, lambda *_: (0,) * nd)


def _rms(v, g):
    return v * lax.rsqrt(jnp.mean(v * v, axis=-1, keepdims=True) + EPS) * g


def _split_bf16(v):
    hi = v.astype(BF16)
    lo = (v - hi.astype(F32)).astype(BF16)
    return hi, lo


def _group_matrix(group):
    r = lax.broadcasted_iota(jnp.int32, (LANES, LANES), 0) // group
    c = lax.broadcasted_iota(jnp.int32, (LANES, LANES), 1) // group
    return jnp.where(r == c, 1.0, 0.0).astype(BF16)


def _group_norm_slab(xs, gmat, inv_n):
    hi, lo = _split_bf16(xs * xs)
    ss = _dot(hi, gmat) + _dot(lo, gmat)
    return xs * lax.rsqrt(ss * inv_n + EPS)


def _inproj_kernel(x_ref, g1_ref, wa_ref, wb_ref, wc_ref, wd_ref, wg_ref,
                   gqa_ref, wqb_ref, gkva_ref, wkvk_ref, wkvv_ref,
                   rc_ref, rs1_ref, rs2_ref,
                   gmq_ref, gmk_ref, gsq_ref, gsk_ref, gdq_ref, gdk_ref, oq_ref, ok_ref, ov_ref,
                   sbq, sbk, sbv, mq, mk, mv, swq, swk, swv,
                   dq0, dq1, dq2, dk0, dk1, dk2, dv0, dv1, dv2, glow):
    x = x_ref[...]
    h = _rms(x, g1_ref[...]).astype(BF16)

    pa = _dot(h, wa_ref[...])
    sbq[...] = pa[:, :SB_W].astype(BF16)
    sbk[...] = pa[:, SB_W:2 * SB_W].astype(BF16)
    sbv[...] = pa[:, 2 * SB_W:].astype(BF16)

    pb = _dot(h, wb_ref[...])
    cq = pb[:, :MLA_Q_RANK]
    ckv = pb[:, MLA_Q_RANK:MLA_Q_RANK + MLA_KV_RANK]
    kpe = pb[:, MLA_Q_RANK + MLA_KV_RANK:]
    width = MLA_HEADS * MLA_PAD
    rc = jnp.concatenate([rc_ref[...]] * MLA_HEADS, axis=1)
    rs1 = jnp.concatenate([rs1_ref[...]] * MLA_HEADS, axis=1)
    rs2 = jnp.concatenate([rs2_ref[...]] * MLA_HEADS, axis=1)
    half = MLA_ROPE // 2

    def rope(v):
        return v * rc + pltpu.roll(v, half, 1) * rs1 + pltpu.roll(v, width - half, 1) * rs2

    ones = _group_matrix(LANES)
    q = rope(_dot(_rms(cq, gqa_ref[...]).astype(BF16), wqb_ref[...]))
    kvn = _rms(ckv, gkva_ref[...]).astype(BF16)
    k = _dot(kvn, wkvk_ref[...]) + rope(kpe)
    mv[...] = (_dot(kvn, wkvv_ref[...]) + ov_ref[...]).astype(BF16)
    gmq = gmq_ref[...]
    gmk = gmk_ref[...]
    oq = oq_ref[...]
    ok = ok_ref[...]
    for s in range(MLA_HEADS):
        sl = slice(s * LANES, (s + 1) * LANES)
        mq[:, sl] = (_group_norm_slab(q[:, sl], ones, 1.0 / MLA_QK) * gmq[:, sl] + oq[:, sl]).astype(BF16)
        mk[:, sl] = (_group_norm_slab(k[:, sl], ones, 1.0 / MLA_QK) * gmk[:, sl] + ok[:, sl]).astype(BF16)

    g32 = _group_matrix(SW_DIM)
    pc = _dot(h, wc_ref[...])
    gsq = gsq_ref[...]
    gsk = gsk_ref[...]
    for s in range(SW_W // LANES):
        sl = slice(s * LANES, (s + 1) * LANES)
        ks = slice(SW_W + s * LANES, SW_W + (s + 1) * LANES)
        swq[:, sl] = (_group_norm_slab(pc[:, sl], g32, 1.0 / SW_DIM) * gsq[:, sl]).astype(BF16)
        swk[:, sl] = (_group_norm_slab(pc[:, ks], g32, 1.0 / SW_DIM) * gsk[:, sl]).astype(BF16)
    swv[...] = pc[:, 2 * SW_W:].astype(BF16)

    pd = _dot(h, wd_ref[...])
    gdq = gdq_ref[...]
    gdk = gdk_ref[...]
    for g, (q_out, k_out, v_out) in enumerate(((dq0, dk0, dv0), (dq1, dk1, dv1), (dq2, dk2, dv2))):
        sl = slice(g * LANES, (g + 1) * LANES)
        ks = slice(DIL_W + g * LANES, DIL_W + (g + 1) * LANES)
        vs = slice(2 * DIL_W + g * LANES, 2 * DIL_W + (g + 1) * LANES)
        q_out[...] = (_group_norm_slab(pd[:, sl], g32, 1.0 / DIL_DIM) * gdq[:, sl]).astype(BF16)
        k_out[...] = (_group_norm_slab(pd[:, ks], g32, 1.0 / DIL_DIM) * gdk[:, sl]).astype(BF16)
        v_out[...] = pd[:, vs].astype(BF16)

    glow[...] = _dot(h, wg_ref[...]).astype(BF16)


def _inproj(x2, g1, wa, wb, wc, wd, wg, gqa, wqb, gkva, wkvk, wkvv, rc, rs1, rs2,
            gmq, gmk, gsq, gsk, gdq, gdk, oq, ok, ov, seq):
    n = x2.shape[0]
    tm = ROW_TILE
    n_seq_tiles = seq // tm
    row = lambda w: pl.BlockSpec((tm, w), lambda i: (i, 0))
    pos = lambda w: pl.BlockSpec((tm, w), lambda i: (i % n_seq_tiles, 0))
    weights = (g1, wa, wb, wc, wd, wg, gqa, wqb, gkva, wkvk, wkvv)
    gains = (gmq, gmk, gsq, gsk, gdq, gdk, oq, ok, ov)
    out_widths = (SB_W,) * 3 + (MLA_HEADS * MLA_PAD,) * 3 + (SW_W,) * 3 \
        + (DIL_GW,) * 9 + (GATE_RANK,)
    return pl.pallas_call(
        _inproj_kernel,
        grid=(n // tm,),
        in_specs=[row(D_MODEL)] + [_full(w.shape) for w in weights] + [pos(LANES)] * 3
        + [_full(g.shape) for g in gains],
        out_specs=[row(w) for w in out_widths],
        out_shape=[jax.ShapeDtypeStruct((n, w), BF16) for w in out_widths],
        compiler_params=_params(("parallel",)),
        name="inproj",
    )(x2, *weights, rc, rs1, rs2, *gains)


def _sb_kernel(q_ref, k_ref, v_ref, o_ref, acc_ref, car_ref):
    t = ATT_TILE
    i = pl.program_id(2)
    q = q_ref[0]
    lane = lax.broadcasted_iota(jnp.int32, (t, LANES), 1)
    zero = jnp.zeros_like(q)
    qs = jnp.concatenate([jnp.where(lane < SB_DIM, q, zero), jnp.where(lane >= SB_DIM, q, zero)], axis=0)
    r = lax.broadcasted_iota(jnp.int32, (t, t), 0)
    c = lax.broadcasted_iota(jnp.int32, (t, t), 1)
    suffix = jnp.where(r >= c, 1.0, 0.0).astype(BF16)
    r2 = lax.broadcasted_iota(jnp.int32, (2 * t, t), 0) & (t - 1)
    c2 = lax.broadcasted_iota(jnp.int32, (2 * t, t), 1)
    strict = c2 < r2

    acc_ref[...] = jnp.zeros_like(acc_ref)
    car_ref[...] = jnp.zeros_like(car_ref)

    def block(j, diagonal):
        off = pl.multiple_of(j * t, t)
        kb = k_ref[0, pl.ds(off, t), :]
        vb = v_ref[0, pl.ds(off, t), :]
        z = _dot_nt(qs, kb)
        lk = -(jnp.maximum(z, 0.0) + jnp.log(1.0 + jnp.exp(-jnp.abs(z))))
        if diagonal:
            lk = jnp.where(strict, lk, 0.0)
        within = _dot(lk.astype(BF16), suffix)
        car = car_ref[...]
        a = jnp.exp(jnp.minimum(z + within, 0.0) + car)
        if diagonal:
            a = jnp.where(strict, a, 0.0)
        acc_ref[...] += _dot(a.astype(BF16), vb)
        car_ref[...] = car + within[:, :1]

    block(i, True)

    def live(s):
        return jnp.logical_and(s < i, jnp.max(car_ref[...]) > SB_LOG_FLOOR)

    def body(s):
        block(i - 1 - s, False)
        return s + 1

    lax.while_loop(live, body, 0)
    o_ref[0] = jnp.where(lane < SB_DIM, acc_ref[:t], acc_ref[t:]).astype(o_ref.dtype)


def _sb_attention(q, k, v):
    b, s, _ = q.shape
    t = ATT_TILE
    return pl.pallas_call(
        _sb_kernel,
        grid=(b, SB_W // LANES, s // t),
        in_specs=[pl.BlockSpec((1, t, LANES), lambda bi, p, i: (bi, i, p)),
                  pl.BlockSpec((1, s, LANES), lambda bi, p, i: (bi, 0, p)),
                  pl.BlockSpec((1, s, LANES), lambda bi, p, i: (bi, 0, p))],
        out_specs=pl.BlockSpec((1, t, LANES), lambda bi, p, i: (bi, i, p)),
        out_shape=jax.ShapeDtypeStruct((b, s, SB_W), BF16),
        scratch_shapes=[pltpu.VMEM((2 * t, LANES), F32), pltpu.VMEM((2 * t, 1), F32)],
        compiler_params=_params(("parallel", "parallel", "arbitrary")),
        name="stick_breaking",
    )(q, k, v)


def _mla_kernel(q_ref, k_ref, v_ref, o_ref, acc_ref, *stat_refs, online):
    t = MLA_TILE
    i = pl.program_id(2)
    q = q_ref[0]
    r = lax.broadcasted_iota(jnp.int32, (t, t), 0)
    c = lax.broadcasted_iota(jnp.int32, (t, t), 1)
    causal = c <= r
    lane = lax.broadcasted_iota(jnp.int32, (t, LANES), 1)
    reps = t // LANES

    acc_ref[...] = jnp.zeros_like(acc_ref)
    if online:
        m_ref = stat_refs[0]
        m_ref[...] = jnp.full_like(m_ref, NEG)

    def block(j, diagonal):
        off = pl.multiple_of(j * t, t)
        kb = k_ref[0, pl.ds(off, t), :]
        vb = v_ref[0, pl.ds(off, t), :]
        for h in range(2):
            sl = slice(h * MLA_PAD, (h + 1) * MLA_PAD)
            s = _dot_nt(q[:, sl], kb[:, sl])
            if diagonal:
                s = jnp.where(causal, s, NEG)
            if online:
                m_old = m_ref[h]
                m_new = jnp.maximum(m_old, jnp.max(s, axis=-1, keepdims=True))
                p = jnp.exp2(s - jnp.tile(m_new, (1, reps)))
                acc_ref[h] = jnp.exp2(m_old - m_new) * acc_ref[h] + _dot(p.astype(BF16), vb[:, sl])
                m_ref[h] = m_new
            else:
                acc_ref[h] += _dot(jnp.exp2(s).astype(BF16), vb[:, sl])

    block(i, True)

    def body(s, carry):
        block(i - 1 - s, False)
        return carry

    lax.fori_loop(0, i, body, 0)
    a0, a1 = acc_ref[0], acc_ref[1]
    half = LANES // 2
    o0 = a0 / pltpu.roll(a0, half, 1)
    o1 = pltpu.roll(a1, half, 1) / a1
    o_ref[0] = jnp.where(lane < MLA_V, o0, o1).astype(o_ref.dtype)


def _mla_attention(q, k, v, bounded):
    b, s, _ = q.shape
    t = MLA_TILE
    stat = pltpu.VMEM((2, t, LANES), F32)

    def call(online):
        return pl.pallas_call(
            functools.partial(_mla_kernel, online=online),
            grid=(b, MLA_HEADS // 2, s // t),
            in_specs=[pl.BlockSpec((1, t, 2 * MLA_PAD), lambda bi, p, i: (bi, i, p)),
                      pl.BlockSpec((1, s, 2 * MLA_PAD), lambda bi, p, i: (bi, 0, p)),
                      pl.BlockSpec((1, s, 2 * MLA_PAD), lambda bi, p, i: (bi, 0, p))],
            out_specs=pl.BlockSpec((1, t, LANES), lambda bi, p, i: (bi, i, p)),
            out_shape=jax.ShapeDtypeStruct((b, s, MLA_HEADS * MLA_V), BF16),
            scratch_shapes=[stat, stat] if online else [stat],
            compiler_params=_params(("parallel", "parallel", "arbitrary")),
            name="latent_attention_online" if online else "latent_attention",
        )

    return lax.cond(bounded, call(False), call(True), q, k, v)


def _band_kernel(*refs, heads, inclusive, with_sinks, with_lse):
    q_ref, kp_ref, kc_ref, vp_ref, vc_ref, bias_ref = refs[:6]
    rest = refs[6:]
    if with_sinks:
        sink_ref, rest = rest[0], rest[1:]
    o_ref = rest[0]
    width = heads * DIL_DIM
    cols = heads * BLOCK
    n = pl.program_id(2)
    k_all = jnp.concatenate([kp_ref[0], kc_ref[0]], axis=0)
    v_all = jnp.concatenate([vp_ref[0], vc_ref[0]], axis=0)
    kj = lax.broadcasted_iota(jnp.int32, (2 * BLOCK, cols), 0)
    qi = lax.broadcasted_iota(jnp.int32, (2 * BLOCK, cols), 1) & (BLOCK - 1)
    near = (kj >= qi) if inclusive else (kj > qi)
    in_window = near & (kj <= qi + BLOCK)
    first_mask = in_window & ((kj >= BLOCK) | (n > 0))
    group = lax.broadcasted_iota(jnp.int32, (BLOCK, width), 1) // DIL_DIM
    bias = bias_ref[...]
    for j in range(BAND_BLOCKS):
        q = q_ref[0, j * BLOCK:(j + 1) * BLOCK, :]
        zero = jnp.zeros_like(q)
        qs = jnp.concatenate([jnp.where(group == h, q, zero) for h in range(heads)], axis=0)
        k = k_all[j * BLOCK:(j + 2) * BLOCK]
        v = v_all[j * BLOCK:(j + 2) * BLOCK]
        s = _dot_nt(k, qs) + bias
        s = jnp.where(first_mask if j == 0 else in_window, s, NEG)
        m = jnp.max(s, axis=0, keepdims=True)
        if with_sinks:
            sink = sink_ref[...]
            m = jnp.maximum(m, sink)
        p = jnp.exp(s - m)
        l = jnp.sum(p, axis=0, keepdims=True)
        if with_sinks:
            l = l + jnp.exp(sink - m)
        o = _dot_tn(v, p.astype(BF16)) * (1.0 / l)
        out = jnp.concatenate([o[h * DIL_DIM:(h + 1) * DIL_DIM, h * BLOCK:(h + 1) * BLOCK]
                               for h in range(heads)], axis=0)
        o_ref[0, j * BLOCK:(j + 1) * BLOCK, :] = out.T.astype(o_ref.dtype)
        if with_lse:
            ls = m + jnp.log(l)
            lse = jnp.concatenate([jnp.broadcast_to(ls[:, h * BLOCK:(h + 1) * BLOCK], (DIL_DIM, BLOCK))
                                   for h in range(heads)], axis=0)
            rest[1][0, j * BLOCK:(j + 1) * BLOCK, :] = lse.T


def _band_attention(q, k, v, bias, sinks, dil, heads, inclusive, with_lse, out_dtype):
    b, s, width = q.shape
    m = s // dil
    step = BAND_BLOCKS * BLOCK
    nb = m // step
    view = lambda a: a.reshape(b, m, dil * width)
    cur = pl.BlockSpec((1, step, width), lambda bi, r, n: (bi, n, r))
    prev = pl.BlockSpec((1, BLOCK, width), lambda bi, r, n: (bi, jnp.maximum(n * BAND_BLOCKS - 1, 0), r))
    bias = bias.transpose(2, 0, 1).reshape(2 * BLOCK, heads * BLOCK)
    in_specs = [cur, prev, cur, prev, cur, _full(bias.shape)]
    args = [view(q), view(k), view(k), view(v), view(v), bias]
    if sinks is not None:
        sink_cols = jnp.repeat(sinks, BLOCK)[None, :]
        in_specs.append(_full(sink_cols.shape))
        args.append(sink_cols)
    out_shape = [jax.ShapeDtypeStruct((b, m, dil * width), out_dtype)]
    out_specs = [cur]
    if with_lse:
        out_shape.append(jax.ShapeDtypeStruct((b, m, dil * width), F32))
        out_specs.append(cur)
    outs = pl.pallas_call(
        functools.partial(_band_kernel, heads=heads, inclusive=inclusive,
                          with_sinks=sinks is not None, with_lse=with_lse),
        grid=(b, dil, nb),
        in_specs=in_specs,
        out_specs=out_specs,
        out_shape=out_shape,
        compiler_params=_params(("parallel", "parallel", "arbitrary")),
        name="band_attention",
    )(*args)
    return [o.reshape(b, s, width) for o in outs]


def _bias_kernel(rb_ref, bkt_ref, o_ref, *, head_patterns):
    for head, pat in enumerate(head_patterns):
        bkt = bkt_ref[pat]
        acc = jnp.zeros(bkt.shape, F32)
        for bucket in range(REL_BUCKETS):
            acc = jnp.where(bkt == bucket, rb_ref[bucket, head], acc)
        o_ref[head] = acc


def _t5_bucket(dist):
    max_exact = REL_BUCKETS // 2
    d = jnp.maximum(dist, 1).astype(F32)
    large = max_exact + (jnp.log(d / max_exact) / math.log(REL_MAX_DIST / max_exact)
                         * (REL_BUCKETS - max_exact)).astype(jnp.int32)
    large = jnp.minimum(large, REL_BUCKETS - 1)
    return jnp.where(dist < max_exact, dist, large)


def _bias_tables(rel_bias):
    dist = (BLOCK + jnp.arange(BLOCK))[:, None] - jnp.arange(2 * BLOCK)[None, :]
    dist = jnp.maximum(dist, 0)
    buckets = jnp.stack([_t5_bucket(dist * dil) for _, dil in DIL_PATTERNS]).astype(jnp.int32)
    head_patterns = (0,) * SW_Q_HEADS + tuple(g for g in range(N_DIL) for _ in range(DIL_HEADS))
    n_heads = len(head_patterns)
    return pl.pallas_call(
        functools.partial(_bias_kernel, head_patterns=head_patterns),
        in_specs=[pl.BlockSpec(memory_space=pltpu.SMEM), _full(buckets.shape)],
        out_specs=_full((n_heads, BLOCK, 2 * BLOCK)),
        out_shape=jax.ShapeDtypeStruct((n_heads, BLOCK, 2 * BLOCK), F32),
        grid=(1,),
        name="rel_bias",
    )(rel_bias, buckets)


def _merge_kernel(*refs, with_router):
    (x_ref, glow_ref, oa_ref, ob_ref, oc_ref, d0_ref, d1_ref, d2_ref, l0_ref, l1_ref, l2_ref,
     wgb_ref, bg_ref, wbr_ref, wout_ref, g2_ref) = refs[:16]
    rest = refs[16:]
    if with_router:
        wr_hi_ref, wr_lo_ref, br_ref = rest[:3]
        rest = rest[3:]
    x1_ref, h2_ref = rest[:2]

    l0, l1, l2 = l0_ref[...], l1_ref[...], l2_ref[...]
    mx = jnp.maximum(jnp.maximum(l0, l1), l2)
    e0, e1, e2 = jnp.exp(l0 - mx), jnp.exp(l1 - mx), jnp.exp(l2 - mx)
    od = (e0 * d0_ref[...] + e1 * d1_ref[...] + e2 * d2_ref[...]) / (e0 + e1 + e2)

    glow = glow_ref[...]
    branches = (oa_ref[...], ob_ref[...], oc_ref[...], od.astype(BF16))
    y = None
    row = 0
    for i, o in enumerate(branches):
        cols = slice(i * D_MODEL, (i + 1) * D_MODEL)
        pre = _dot(glow, wgb_ref[:, cols]) + bg_ref[:, cols]
        gate = 1.0 / (1.0 + jnp.exp(-pre))
        term = gate * _dot(o, wbr_ref[row:row + BR_SIZES[i], :])
        y = term if y is None else y + term
        row += BR_SIZES[i]
    x1 = x_ref[...] + _dot(y.astype(BF16), wout_ref[...])
    x1_ref[...] = x1
    h2 = _rms(x1, g2_ref[...])
    h2_ref[...] = h2.astype(BF16)

    if with_router:
        wts_ref = rest[2]
        hi, lo = _split_bf16(h2)
        whi = wr_hi_ref[...]
        lg = _dot(hi, whi) + _dot(lo, whi) + _dot(hi, wr_lo_ref[...]) + br_ref[...]
        lane = lax.broadcasted_iota(jnp.int32, lg.shape, 1).astype(F32)
        big = float(LANES)
        m1 = jnp.max(lg, axis=-1, keepdims=True)
        i1 = jnp.min(jnp.where(lg == m1, lane, big), axis=-1, keepdims=True)
        lg2 = jnp.where(lane == i1, NEG, lg)
        m2 = jnp.max(lg2, axis=-1, keepdims=True)
        i2 = jnp.min(jnp.where(lg2 == m2, lane, big), axis=-1, keepdims=True)
        e = jnp.exp(m2 - m1)
        w1 = 1.0 / (1.0 + e)
        w2 = e / (1.0 + e)
        wts_ref[...] = jnp.where(lane == i1, w1, 0.0) + jnp.where(lane == i2, w2, 0.0)


def _merge(x2, glow, oa, ob, oc, dil_o, dil_l, wgb, bg, wbr, wout, g2, router):
    n = x2.shape[0]
    tm = ROW_TILE
    row = lambda w: pl.BlockSpec((tm, w), lambda i: (i, 0))
    acts = [x2, glow, oa, ob, oc, *dil_o, *dil_l]
    weights = [wgb, bg, wbr, wout, g2] + (list(router) if router is not None else [])
    out_shape = [jax.ShapeDtypeStruct((n, D_MODEL), F32), jax.ShapeDtypeStruct((n, D_MODEL), BF16)]
    out_specs = [row(D_MODEL), row(D_MODEL)]
    if router is not None:
        out_shape.append(jax.ShapeDtypeStruct((n, LANES), F32))
        out_specs.append(row(LANES))
    return pl.pallas_call(
        functools.partial(_merge_kernel, with_router=router is not None),
        grid=(n // tm,),
        in_specs=[row(a.shape[1]) for a in acts] + [_full(w.shape) for w in weights],
        out_specs=out_specs,
        out_shape=out_shape,
        compiler_params=_params(("parallel",)),
        name="merge",
    )(*acts, *weights)


FF_CHUNK = 512


def _dense_ffn_kernel(x1_ref, h2_ref, wg_ref, wu_ref, wd_ref, o_ref):
    h2 = h2_ref[...]
    acc = x1_ref[...]
    for c in range(D_FF_DENSE // FF_CHUNK):
        cols = slice(c * FF_CHUNK, (c + 1) * FF_CHUNK)
        g = _dot(h2, wg_ref[:, cols])
        u = _dot(h2, wu_ref[:, cols])
        act = (g / (1.0 + jnp.exp(-g)) * u).astype(BF16)
        acc = acc + _dot(act, wd_ref[cols, :])
    o_ref[...] = acc


def _dense_ffn(x1, h2, wg, wu, wd):
    n = x1.shape[0]
    tm = ROW_TILE
    row = pl.BlockSpec((tm, D_MODEL), lambda i: (i, 0))
    return pl.pallas_call(
        _dense_ffn_kernel,
        grid=(n // tm,),
        in_specs=[row, row, _full(wg.shape), _full(wu.shape), _full(wd.shape)],
        out_specs=row,
        out_shape=jax.ShapeDtypeStruct((n, D_MODEL), F32),
        compiler_params=_params(("parallel",)),
        name="dense_ffn",
    )(x1, h2, wg, wu, wd)


def _moe_kernel(x1_ref, h2_ref, wts_ref, wg_ref, wu_ref, wd_ref, o_ref, acc_ref):
    e = pl.program_id(1)

    @pl.when(e == 0)
    def _():
        acc_ref[...] = x1_ref[...]

    wts = wts_ref[...]
    lane = lax.broadcasted_iota(jnp.int32, wts.shape, 1)
    w_e = jnp.sum(jnp.where(lane == e, wts, 0.0), axis=-1, keepdims=True)
    h2 = h2_ref[...]
    g = _dot(h2, wg_ref[0])
    u = _dot(h2, wu_ref[0])
    act = (g / (1.0 + jnp.exp(-g)) * u).astype(BF16)
    acc_ref[...] += w_e * _dot(act, wd_ref[0])

    @pl.when(e == N_EXPERTS - 1)
    def _():
        o_ref[...] = acc_ref[...]


def _moe_ffn(x1, h2, wts, wg, wu, wd):
    n = x1.shape[0]
    tm = ROW_TILE
    row = lambda w: pl.BlockSpec((tm, w), lambda i, e: (i, 0))
    return pl.pallas_call(
        _moe_kernel,
        grid=(n // tm, N_EXPERTS),
        in_specs=[row(D_MODEL), row(D_MODEL), row(LANES),
                  pl.BlockSpec((1, D_MODEL, D_FF_EXPERT), lambda i, e: (e, 0, 0)),
                  pl.BlockSpec((1, D_MODEL, D_FF_EXPERT), lambda i, e: (e, 0, 0)),
                  pl.BlockSpec((1, D_FF_EXPERT, D_MODEL), lambda i, e: (e, 0, 0))],
        out_specs=row(D_MODEL),
        out_shape=jax.ShapeDtypeStruct((n, D_MODEL), F32),
        scratch_shapes=[pltpu.VMEM((tm, D_MODEL), F32)],
        compiler_params=_params(("parallel", "arbitrary")),
        name="moe_ffn",
    )(x1, h2, wts, wg, wu, wd)


def _rope_tables(seq):
    half = MLA_ROPE // 2
    inv = ROPE_THETA ** (-jnp.arange(half, dtype=F32) / half)
    ang = jnp.arange(seq, dtype=F32)[:, None] * inv[None, :]
    cos, sin = jnp.cos(ang), jnp.sin(ang)
    zeros = lambda w: jnp.zeros((seq, w), F32)
    tail = LANES - MLA_QK
    rc = jnp.concatenate([jnp.ones((seq, MLA_NOPE), F32), cos, cos, zeros(tail)], axis=1)
    rs1 = jnp.concatenate([zeros(MLA_NOPE + half), sin, zeros(tail)], axis=1)
    rs2 = jnp.concatenate([zeros(MLA_NOPE), -sin, zeros(half + tail)], axis=1)
    return rc, rs1, rs2


def _pad_heads(w, heads, dim):
    rows = w.shape[0]
    w = w.reshape(rows, heads, dim)
    return jnp.pad(w, ((0, 0), (0, 0), (0, LANES - dim))).reshape(rows, heads * LANES)


def _layer_weights(w_in, g_qa, w_qb, g_kva, w_kvb, qk_g_mla, qk_g_sw, qk_g_dil, w_gate_a):
    offs = np.concatenate([[0], np.cumsum(IN_SIZES)]).tolist()
    cols = [w_in[:, offs[j]:offs[j + 1]] for j in range(len(IN_SIZES))]
    a_q, a_k, a_v, b_cq, b_ckv, b_kpe, c_q, c_k, c_v, d_q, d_k, d_v = cols
    wa = jnp.concatenate([a_q * (SB_DIM ** -0.5), a_k, a_v], axis=1)
    kpe_pad = jnp.pad(b_kpe, ((0, 0), (MLA_NOPE, LANES - MLA_QK)))
    wb = jnp.concatenate([b_cq, b_ckv] + [kpe_pad] * MLA_HEADS, axis=1)
    rep = SW_Q_HEADS // SW_KV_HEADS
    expand = lambda w: jnp.repeat(w.reshape(D_MODEL, SW_KV_HEADS, SW_DIM), rep, axis=1).reshape(D_MODEL, SW_W)
    wc = jnp.concatenate([c_q, expand(c_k), expand(c_v)], axis=1)
    wd = jnp.concatenate([d_q, d_k, d_v], axis=1)
    wqb = _pad_heads(w_qb, MLA_HEADS, MLA_QK)
    kvb = w_kvb.reshape(MLA_KV_RANK, MLA_HEADS, MLA_NOPE + MLA_V)
    wkvk = _pad_heads(kvb[:, :, :MLA_NOPE].reshape(MLA_KV_RANK, -1), MLA_HEADS, MLA_NOPE)
    wkvv = _pad_heads(kvb[:, :, MLA_NOPE:].reshape(MLA_KV_RANK, -1), MLA_HEADS, MLA_V)
    pad_gain = lambda g: jnp.tile(jnp.pad(g, (0, LANES - MLA_QK)), MLA_HEADS)[None, :]
    gmq = pad_gain(qk_g_mla[0]) * (MLA_QK ** -0.5 * math.log2(math.e))
    gmk = pad_gain(qk_g_mla[1])
    bound = jnp.max(jnp.abs(qk_g_mla[0] * qk_g_mla[1])) * (MLA_QK ** 0.5 * math.log2(math.e))
    shift = (bound * MLA_BOUND_MARGIN).astype(BF16).astype(F32)
    bounded = shift < MLA_MAX_SHIFT
    spare = lambda lo, hi: jnp.tile(jnp.pad(jnp.ones((hi - lo,), F32), (lo, LANES - hi)), MLA_HEADS)[None, :]
    oq = jnp.where(bounded, -shift, 0.0) * spare(MLA_QK, MLA_QK + 1)
    ok = spare(MLA_QK, MLA_QK + 1)
    ov = spare(MLA_V, LANES)
    gsq = jnp.tile(qk_g_sw[0], SW_Q_HEADS)[None, :] * (SW_DIM ** -0.5)
    gsk = jnp.tile(qk_g_sw[1], SW_Q_HEADS)[None, :]
    gdq = jnp.tile(qk_g_dil[0], N_DIL * DIL_HEADS)[None, :] * (DIL_DIM ** -0.5)
    gdk = jnp.tile(qk_g_dil[1], N_DIL * DIL_HEADS)[None, :]
    bf = lambda w: w.astype(BF16)
    return (bf(wa), bf(wb), bf(wc), bf(wd), bf(w_gate_a), g_qa[None, :], bf(wqb), g_kva[None, :],
            bf(wkvk), bf(wkvv)), (gmq, gmk, gsq, gsk, gdq, gdk, oq, ok, ov), bounded


def kernel(x, norm1_g, w_in, g_qa, w_qb, g_kva, w_kvb, qk_g_mla, qk_g_sw, qk_g_dil, sinks, rel_bias, w_gate_a, w_gate_b, b_gate, w_branch, w_out, norm2_g, w_gu_dense, w_down_dense, w_router, b_router, w_gu_exp, w_down_exp):
    b, s, d = x.shape
    n = b * s
    depth = norm1_g.shape[0]
    bias = _bias_tables(rel_bias)
    sw_bias = bias[:SW_Q_HEADS]
    dil_bias = [bias[SW_Q_HEADS + g * DIL_HEADS:SW_Q_HEADS + (g + 1) * DIL_HEADS] for g in range(N_DIL)]
    rc, rs1, rs2 = _rope_tables(s)
    x2 = x.reshape(n, d)
    seq3 = lambda a: a.reshape(b, s, a.shape[-1])
    flat = lambda a: a.reshape(n, a.shape[-1])

    for i in range(depth):
        weights, gains, bounded = _layer_weights(w_in[i], g_qa[i], w_qb[i], g_kva[i], w_kvb[i], qk_g_mla[i],
                                        qk_g_sw[i], qk_g_dil[i], w_gate_a[i])
        wa, wb, wc, wd, wg, gqa, wqb, gkva, wkvk, wkvv = weights
        (sbq, sbk, sbv, mq, mk, mv, swq, swk, swv,
         dq0, dq1, dq2, dk0, dk1, dk2, dv0, dv1, dv2, glow) = _inproj(
            x2, norm1_g[i][None, :], wa, wb, wc, wd, wg, gqa, wqb, gkva, wkvk, wkvv, rc, rs1, rs2,
            *gains, s)

        out_a = _sb_attention(seq3(sbq), seq3(sbk), seq3(sbv))
        out_b = _mla_attention(seq3(mq), seq3(mk), seq3(mv), bounded)
        (out_c,) = _band_attention(seq3(swq), seq3(swk), seq3(swv), sw_bias, sinks[i], 1,
                                   SW_Q_HEADS, False, False, BF16)
        dil_o, dil_l = [], []
        for g, (dq, dk, dv) in enumerate(((dq0, dk0, dv0), (dq1, dk1, dv1), (dq2, dk2, dv2))):
            o, l = _band_attention(seq3(dq), seq3(dk), seq3(dv), dil_bias[g], None,
                                   DIL_PATTERNS[g][1], DIL_HEADS, True, True, F32)
            dil_o.append(flat(o))
            dil_l.append(flat(l))

        router = None
        if i % 2 == 1:
            wr = jnp.pad(w_router[i // 2], ((0, 0), (0, LANES - N_EXPERTS)))
            wr_hi = wr.astype(BF16)
            wr_lo = (wr - wr_hi.astype(F32)).astype(BF16)
            br = jnp.pad(b_router[i // 2], (0, LANES - N_EXPERTS), constant_values=NEG)[None, :]
            router = (wr_hi, wr_lo, br)
        merged = _merge(x2, glow, flat(out_a), flat(out_b), flat(out_c), dil_o, dil_l,
                        w_gate_b[i].astype(BF16), b_gate[i][None, :], w_branch[i].astype(BF16),
                        w_out[i].astype(BF16), norm2_g[i][None, :], router)
        if i % 2 == 0:
            x1, h2 = merged
            w_gu = w_gu_dense[i // 2].astype(BF16)
            x2 = _dense_ffn(x1, h2, w_gu[:, :D_FF_DENSE], w_gu[:, D_FF_DENSE:],
                            w_down_dense[i // 2].astype(BF16))
        else:
            x1, h2, wts = merged
            w_gu = w_gu_exp[i // 2].astype(BF16)
            x2 = _moe_ffn(x1, h2, wts, w_gu[:, :, :D_FF_EXPERT], w_gu[:, :, D_FF_EXPERT:],
                          w_down_exp[i // 2].astype(BF16))
    return x2.reshape(b, s, d)
```

```python
import functools
import math

import jax
import jax.numpy as jnp
import numpy as np
from jax import lax
from jax.experimental import pallas as pl
from jax.experimental.pallas import tpu as pltpu

F32 = jnp.float32
BF16 = jnp.bfloat16

D_MODEL = 1024
BLOCK = 128
EPS = 1e-6
SB_HEADS, SB_DIM = 4, 64
MLA_HEADS, MLA_Q_RANK, MLA_KV_RANK = 4, 256, 256
MLA_NOPE, MLA_ROPE, MLA_V = 64, 32, 64
MLA_QK = MLA_NOPE + MLA_ROPE
ROPE_THETA = 10000.0
SW_Q_HEADS, SW_KV_HEADS, SW_DIM = 8, 2, 32
DIL_PATTERNS = ((128, 1), (512, 4), (2048, 16))
DIL_HEADS, DIL_DIM = 4, 32
N_DIL = len(DIL_PATTERNS)
REL_BUCKETS, REL_MAX_DIST = 32, 2048
GATE_RANK, N_BRANCH = 128, 4
D_FF_DENSE = 2048
N_EXPERTS, TOP_K, D_FF_EXPERT = 8, 2, 768

SB_W = SB_HEADS * SB_DIM
SW_W = SW_Q_HEADS * SW_DIM
SW_KV_W = SW_KV_HEADS * SW_DIM
DIL_GW = DIL_HEADS * DIL_DIM
DIL_W = N_DIL * DIL_GW
IN_SIZES = (SB_W, SB_W, SB_W, MLA_Q_RANK, MLA_KV_RANK, MLA_ROPE, SW_W, SW_KV_W, SW_KV_W, DIL_W, DIL_W, DIL_W)
BR_SIZES = (SB_W, MLA_HEADS * MLA_V, SW_W, DIL_GW)

LANES = 128
MLA_PAD = LANES
NEG = -1e30
VMEM_LIMIT = 56 * 1024 * 1024

ROW_TILE = 512
ATT_TILE = 256
MLA_Q_TILE = 2048
MLA_K_TILE = 512
SB_LOG_FLOOR = -105.0
MLA_BOUND_MARGIN = 1.03
MLA_MAX_SHIFT = 60.0
BAND_PLAN = {1: (8, 1), 4: (8, 1), 16: (2, 4)}


def _dot(a, b):
    return jnp.dot(a, b, preferred_element_type=F32)


def _dot_nt(a, b):
    return lax.dot_general(a, b, (((1,), (1,)), ((), ())), preferred_element_type=F32)


def _dot_tn(a, b):
    return lax.dot_general(a, b, (((0,), (0,)), ((), ())), preferred_element_type=F32)


def _params(sem):
    return pltpu.CompilerParams(dimension_semantics=sem, vmem_limit_bytes=VMEM_LIMIT)


def _full(shape):
    nd = len(shape)
    return pl.BlockSpec(shape, lambda *_: (0,) * nd)


def _rms(v, g):
    return v * lax.rsqrt(jnp.mean(v * v, axis=-1, keepdims=True) + EPS) * g


def _split_bf16(v):
    hi = v.astype(BF16)
    lo = (v - hi.astype(F32)).astype(BF16)
    return hi, lo


def _group_matrix(group):
    r = (lax.broadcasted_iota(jnp.int32, (2 * LANES, LANES), 0) & (LANES - 1)) // group
    c = lax.broadcasted_iota(jnp.int32, (2 * LANES, LANES), 1) // group
    return jnp.where(r == c, 1.0, 0.0).astype(BF16)


def _group_norm_slab(xs, gmat, inv_n):
    hi, lo = _split_bf16(xs * xs)
    ss = _dot(jnp.concatenate([hi, lo], axis=1), gmat)
    return xs * lax.rsqrt(ss * inv_n + EPS)


def _inproj_kernel(x_ref, g1_ref, wa_ref, wb_ref, wc_ref, wd_ref, wg_ref,
                   gqa_ref, wqb_ref, gkva_ref, wkvk_ref, wkvv_ref,
                   rc_ref, rs1_ref, rs2_ref,
                   gmq_ref, gmk_ref, gsq_ref, gsk_ref, gdq_ref, gdk_ref, oq_ref, ok_ref, ov_ref,
                   sbq, sbk, sbv, mq, mk, mv, swq, swk, swv,
                   dq0, dq1, dq2, dk0, dk1, dk2, dv0, dv1, dv2, glow):
    x = x_ref[...]
    h = _rms(x, g1_ref[...]).astype(BF16)

    pa = _dot(h, wa_ref[...])
    sbq[...] = pa[:, :SB_W].astype(BF16)
    sbk[...] = pa[:, SB_W:2 * SB_W].astype(BF16)
    sbv[...] = pa[:, 2 * SB_W:].astype(BF16)

    pb = _dot(h, wb_ref[...])
    cq = pb[:, :MLA_Q_RANK]
    ckv = pb[:, MLA_Q_RANK:MLA_Q_RANK + MLA_KV_RANK]
    kpe = pb[:, MLA_Q_RANK + MLA_KV_RANK:]
    half = MLA_ROPE // 2

    def rope(v, reps):
        tile = lambda t: jnp.concatenate([t] * reps, axis=1) if reps > 1 else t
        width = reps * LANES
        return (v * tile(rc_ref[...]) + pltpu.roll(v, half, 1) * tile(rs1_ref[...])
                + pltpu.roll(v, width - half, 1) * tile(rs2_ref[...]))

    ones = _group_matrix(LANES)
    q = rope(_dot(_rms(cq, gqa_ref[...]).astype(BF16), wqb_ref[...]), MLA_HEADS)
    kvn = _rms(ckv, gkva_ref[...]).astype(BF16)
    k = _dot(kvn, wkvk_ref[...]) + jnp.concatenate([rope(kpe, 1)] * MLA_HEADS, axis=1)
    mv[...] = (_dot(kvn, wkvv_ref[...]) + ov_ref[...]).astype(BF16)
    gmq = gmq_ref[...]
    gmk = gmk_ref[...]
    oq = oq_ref[...]
    ok = ok_ref[...]
    for s in range(MLA_HEADS):
        sl = slice(s * LANES, (s + 1) * LANES)
        mq[:, sl] = (_group_norm_slab(q[:, sl], ones, 1.0 / MLA_QK) * gmq[:, sl] + oq[:, sl]).astype(BF16)
        mk[:, sl] = (_group_norm_slab(k[:, sl], ones, 1.0 / MLA_QK) * gmk[:, sl] + ok[:, sl]).astype(BF16)

    g32 = _group_matrix(SW_DIM)
    pc = _dot(h, wc_ref[...])
    gsq = gsq_ref[...]
    gsk = gsk_ref[...]
    for s in range(SW_W // LANES):
        sl = slice(s * LANES, (s + 1) * LANES)
        ks = slice(SW_W + s * LANES, SW_W + (s + 1) * LANES)
        swq[:, sl] = (_group_norm_slab(pc[:, sl], g32, 1.0 / SW_DIM) * gsq[:, sl]).astype(BF16)
        swk[:, sl] = (_group_norm_slab(pc[:, ks], g32, 1.0 / SW_DIM) * gsk[:, sl]).astype(BF16)
    swv[...] = pc[:, 2 * SW_W:].astype(BF16)

    pd = _dot(h, wd_ref[...])
    gdq = gdq_ref[...]
    gdk = gdk_ref[...]
    for g, (q_out, k_out, v_out) in enumerate(((dq0, dk0, dv0), (dq1, dk1, dv1), (dq2, dk2, dv2))):
        sl = slice(g * LANES, (g + 1) * LANES)
        ks = slice(DIL_W + g * LANES, DIL_W + (g + 1) * LANES)
        vs = slice(2 * DIL_W + g * LANES, 2 * DIL_W + (g + 1) * LANES)
        q_out[...] = _group_norm_slab(pd[:, sl], g32, 1.0 / DIL_DIM) * gdq[:, sl]
        k_out[...] = _group_norm_slab(pd[:, ks], g32, 1.0 / DIL_DIM) * gdk[:, sl]
        v_out[...] = pd[:, vs]

    glow[...] = _dot(h, wg_ref[...]).astype(BF16)


def _inproj(x2, g1, wa, wb, wc, wd, wg, gqa, wqb, gkva, wkvk, wkvv, rc, rs1, rs2,
            gmq, gmk, gsq, gsk, gdq, gdk, oq, ok, ov, seq):
    n = x2.shape[0]
    tm = ROW_TILE
    n_seq_tiles = seq // tm
    row = lambda w: pl.BlockSpec((tm, w), lambda i: (i, 0))
    pos = lambda w: pl.BlockSpec((tm, w), lambda i: (i % n_seq_tiles, 0))
    weights = (g1, wa, wb, wc, wd, wg, gqa, wqb, gkva, wkvk, wkvv)
    gains = (gmq, gmk, gsq, gsk, gdq, gdk, oq, ok, ov)
    outs = [(SB_W, BF16)] * 3 + [(MLA_HEADS * MLA_PAD, BF16)] * 3 + [(SW_W, BF16)] * 3 \
        + [(DIL_GW, F32)] * 9 + [(GATE_RANK, BF16)]
    return pl.pallas_call(
        _inproj_kernel,
        grid=(n // tm,),
        in_specs=[row(D_MODEL)] + [_full(w.shape) for w in weights] + [pos(LANES)] * 3
        + [_full(g.shape) for g in gains],
        out_specs=[row(w) for w, _ in outs],
        out_shape=[jax.ShapeDtypeStruct((n, w), dt) for w, dt in outs],
        compiler_params=_params(("parallel",)),
        name="inproj",
    )(x2, *weights, rc, rs1, rs2, *gains)


def _sb_kernel(q_ref, k_ref, v_ref, o_ref, acc_ref, car_ref):
    t = ATT_TILE
    i = pl.program_id(2)
    q = q_ref[0]
    lane = lax.broadcasted_iota(jnp.int32, (t, LANES), 1)
    zero = jnp.zeros_like(q)
    qs = jnp.concatenate([jnp.where(lane < SB_DIM, q, zero), jnp.where(lane >= SB_DIM, q, zero)], axis=0)
    r = lax.broadcasted_iota(jnp.int32, (t, t), 0)
    c = lax.broadcasted_iota(jnp.int32, (t, t), 1)
    suffix = jnp.where(r >= c, 1.0, 0.0).astype(BF16)
    r2 = lax.broadcasted_iota(jnp.int32, (2 * t, t), 0) & (t - 1)
    c2 = lax.broadcasted_iota(jnp.int32, (2 * t, t), 1)
    strict = c2 < r2

    acc_ref[...] = jnp.zeros_like(acc_ref)
    car_ref[...] = jnp.zeros_like(car_ref)

    def block(j, diagonal):
        off = pl.multiple_of(j * t, t)
        kb = k_ref[0, pl.ds(off, t), :]
        vb = v_ref[0, pl.ds(off, t), :]
        z = _dot_nt(qs, kb)
        lk = -(jnp.maximum(z, 0.0) + jnp.log(1.0 + jnp.exp(-jnp.abs(z))))
        if diagonal:
            lk = jnp.where(strict, lk, 0.0)
        within = _dot(lk.astype(BF16), suffix)
        car = car_ref[...]
        a = jnp.exp(jnp.minimum(z + within, 0.0) + car)
        if diagonal:
            a = jnp.where(strict, a, 0.0)
        acc_ref[...] += _dot(a.astype(BF16), vb)
        car_ref[...] = car + within[:, :1]

    block(i, True)

    def live(s):
        return jnp.logical_and(s < i, jnp.max(car_ref[...]) > SB_LOG_FLOOR)

    def body(s):
        block(i - 1 - s, False)
        return s + 1

    lax.while_loop(live, body, 0)
    o_ref[0] = jnp.where(lane < SB_DIM, acc_ref[:t], acc_ref[t:]).astype(o_ref.dtype)


def _sb_attention(q, k, v):
    b, s, _ = q.shape
    t = ATT_TILE
    return pl.pallas_call(
        _sb_kernel,
        grid=(b, SB_W // LANES, s // t),
        in_specs=[pl.BlockSpec((1, t, LANES), lambda bi, p, i: (bi, i, p)),
                  pl.BlockSpec((1, s, LANES), lambda bi, p, i: (bi, 0, p)),
                  pl.BlockSpec((1, s, LANES), lambda bi, p, i: (bi, 0, p))],
        out_specs=pl.BlockSpec((1, t, LANES), lambda bi, p, i: (bi, i, p)),
        out_shape=jax.ShapeDtypeStruct((b, s, SB_W), BF16),
        scratch_shapes=[pltpu.VMEM((2 * t, LANES), F32), pltpu.VMEM((2 * t, 1), F32)],
        compiler_params=_params(("parallel", "parallel", "arbitrary")),
        name="stick_breaking",
    )(q, k, v)


def _mla_kernel(q_ref, k_ref, v_ref, o_ref, acc_ref, *stat_refs, online):
    tq, tk = MLA_Q_TILE, MLA_K_TILE
    sub = tq // tk
    i = pl.program_id(2)
    r = lax.broadcasted_iota(jnp.int32, (tk, tk), 0)
    c = lax.broadcasted_iota(jnp.int32, (tk, tk), 1)
    causal = c <= r
    lane = lax.broadcasted_iota(jnp.int32, (tq, LANES), 1)
    reps = tk // LANES

    acc_ref[...] = jnp.zeros_like(acc_ref)
    if online:
        m_ref = stat_refs[0]
        m_ref[...] = jnp.full_like(m_ref, NEG)

    def block(j, rows, diagonal):
        off = pl.multiple_of(j * tk, tk)
        kb = k_ref[0, pl.ds(off, tk), :]
        vb = v_ref[0, pl.ds(off, tk), :]
        for h in range(2):
            sl = slice(h * MLA_PAD, (h + 1) * MLA_PAD)
            s = _dot_nt(q_ref[0, rows, sl], kb[:, sl])
            if diagonal:
                s = jnp.where(causal, s, NEG)
            if online:
                m_old = m_ref[h, rows, :]
                m_new = jnp.maximum(m_old, jnp.max(s, axis=-1, keepdims=True))
                p = jnp.exp2(s - jnp.tile(m_new, (1, reps)))
                acc_ref[h, rows, :] = (jnp.exp2(m_old - m_new) * acc_ref[h, rows, :]
                                       + _dot(p.astype(BF16), vb[:, sl]))
                m_ref[h, rows, :] = m_new
            else:
                acc_ref[h, rows, :] += _dot(jnp.exp2(s).astype(BF16), vb[:, sl])

    for a in range(sub):
        rows = slice(a * tk, (a + 1) * tk)
        for d in range(a + 1):
            block(i * sub + d, rows, d == a)

    def body(j, carry):
        block(j, slice(0, tq), False)
        return carry

    lax.fori_loop(0, i * sub, body, 0)
    a0, a1 = acc_ref[0], acc_ref[1]
    half = LANES // 2
    o0 = a0 / pltpu.roll(a0, half, 1)
    o1 = pltpu.roll(a1, half, 1) / a1
    o_ref[0] = jnp.where(lane < MLA_V, o0, o1).astype(o_ref.dtype)


def _mla_attention(q, k, v, bounded):
    b, s, _ = q.shape
    t = MLA_Q_TILE
    stat = pltpu.VMEM((2, t, LANES), F32)

    def call(online):
        return pl.pallas_call(
            functools.partial(_mla_kernel, online=online),
            grid=(b, MLA_HEADS // 2, s // t),
            in_specs=[pl.BlockSpec((1, t, 2 * MLA_PAD), lambda bi, p, i: (bi, i, p)),
                      pl.BlockSpec((1, s, 2 * MLA_PAD), lambda bi, p, i: (bi, 0, p)),
                      pl.BlockSpec((1, s, 2 * MLA_PAD), lambda bi, p, i: (bi, 0, p))],
            out_specs=pl.BlockSpec((1, t, LANES), lambda bi, p, i: (bi, i, p)),
            out_shape=jax.ShapeDtypeStruct((b, s, MLA_HEADS * MLA_V), BF16),
            scratch_shapes=[stat, stat] if online else [stat],
            compiler_params=_params(("parallel", "parallel", "arbitrary")),
            name="latent_attention_online" if online else "latent_attention",
        )

    return lax.cond(bounded, call(False), call(True), q, k, v)


def _band_kernel(*refs, heads, dil, blocks, fan, with_sinks, with_lse):
    q_ref, kp_ref, kc_ref, vp_ref, vc_ref, bias_ref, first_ref = refs[:7]
    rest = refs[7:]
    if with_sinks:
        sink_ref, rest = rest[0], rest[1:]
    o_ref = rest[0]
    width = heads * DIL_DIM
    n = pl.program_id(1)
    group = lax.broadcasted_iota(jnp.int32, (BLOCK, width), 1) // DIL_DIM
    bias = bias_ref[...]
    first_bias = jnp.where(n > 0, bias, first_ref[...])

    def rows(r, blk):
        start = r + blk * (BLOCK * dil)
        return pl.ds(start, BLOCK, stride=dil) if dil > 1 else pl.ds(start, BLOCK)

    def load(ref, r, blk):
        return ref[0, rows(r, blk), :].astype(BF16)

    def streams(r0):
        idx, vs, ss, ps, ls, ms = [], [], [], [], [], []
        for r, j in [(r0 + dr, j) for dr in range(fan) for j in range(blocks)]:
            q = load(q_ref, r, j)
            zero = jnp.zeros_like(q)
            qs = jnp.concatenate([jnp.where(group == h, q, zero) for h in range(heads)], axis=0)
            k_prev = load(kp_ref, r, 0) if j == 0 else load(kc_ref, r, j - 1)
            v_prev = load(vp_ref, r, 0) if j == 0 else load(vc_ref, r, j - 1)
            k = jnp.concatenate([k_prev, load(kc_ref, r, j)], axis=0)
            vs.append(jnp.concatenate([v_prev, load(vc_ref, r, j)], axis=0))
            ss.append(_dot_nt(k, qs) + (first_bias if j == 0 else bias))
            idx.append(rows(r, j))
        for s in ss:
            m = jnp.max(s, axis=0, keepdims=True)
            if with_sinks:
                sink = sink_ref[...]
                m = jnp.maximum(m, sink)
            p = jnp.exp2(s - m)
            l = jnp.sum(p, axis=0, keepdims=True)
            if with_sinks:
                l = l + jnp.exp2(sink - m)
            ps.append(p.astype(BF16))
            ls.append(l)
            ms.append(m)
        os_ = [_dot_tn(v, p) * (1.0 / l) for v, p, l in zip(vs, ps, ls)]
        for j, o in enumerate(os_):
            out = jnp.concatenate([o[h * DIL_DIM:(h + 1) * DIL_DIM, h * BLOCK:(h + 1) * BLOCK]
                                   for h in range(heads)], axis=0)
            o_ref[0, idx[j], :] = out.T.astype(o_ref.dtype)
            if with_lse:
                lg = (ms[j] + jnp.log2(ls[j])) * math.log(2.0)
                lse = jnp.concatenate([jnp.broadcast_to(lg[:, h * BLOCK:(h + 1) * BLOCK], (DIL_DIM, BLOCK))
                                       for h in range(heads)], axis=0)
                rest[1][0, idx[j], :] = lse.T

    if dil == fan:
        streams(0)
    else:
        def body(t, carry):
            streams(t * fan)
            return carry
        lax.fori_loop(0, dil // fan, body, 0)


def _band_attention(q, k, v, bias, sinks, dil, heads, inclusive, with_lse, out_dtype):
    b, s, width = q.shape
    blocks, fan = BAND_PLAN[dil]
    step = blocks * BLOCK * dil
    qi = np.arange(BLOCK)[None, :, None]
    kj = np.arange(2 * BLOCK)[None, None, :]
    window = ((kj >= qi) if inclusive else (kj > qi)) & (kj <= qi + BLOCK)
    table = lambda mask: jnp.where(mask, bias * math.log2(math.e), NEG).transpose(2, 0, 1).reshape(
        2 * BLOCK, heads * BLOCK)
    tables = (table(window), table(window & (kj >= BLOCK)))
    cur = pl.BlockSpec((1, step, width), lambda bi, n: (bi, n, 0))
    prev = pl.BlockSpec((1, BLOCK * dil, width), lambda bi, n: (bi, jnp.maximum(n * blocks - 1, 0), 0))
    in_specs = [cur, prev, cur, prev, cur] + [_full(t.shape) for t in tables]
    args = [q, k, k, v, v, *tables]
    if sinks is not None:
        sink_cols = jnp.repeat(sinks * math.log2(math.e), BLOCK)[None, :]
        in_specs.append(_full(sink_cols.shape))
        args.append(sink_cols)
    out_shape = [jax.ShapeDtypeStruct((b, s, width), out_dtype)]
    out_specs = [cur]
    if with_lse:
        out_shape.append(jax.ShapeDtypeStruct((b, s, width), F32))
        out_specs.append(cur)
    return pl.pallas_call(
        functools.partial(_band_kernel, heads=heads, dil=dil, blocks=blocks, fan=fan,
                          with_sinks=sinks is not None, with_lse=with_lse),
        grid=(b, s // step),
        in_specs=in_specs,
        out_specs=out_specs,
        out_shape=out_shape,
        compiler_params=_params(("parallel", "arbitrary")),
        name="band_attention",
    )(*args)


def _bias_kernel(rb_ref, bkt_ref, o_ref, *, head_patterns):
    for head, pat in enumerate(head_patterns):
        bkt = bkt_ref[pat]
        acc = jnp.zeros(bkt.shape, F32)
        for bucket in range(REL_BUCKETS):
            acc = jnp.where(bkt == bucket, rb_ref[bucket, head], acc)
        o_ref[head] = acc


def _t5_bucket(dist):
    max_exact = REL_BUCKETS // 2
    d = jnp.maximum(dist, 1).astype(F32)
    large = max_exact + (jnp.log(d / max_exact) / math.log(REL_MAX_DIST / max_exact)
                         * (REL_BUCKETS - max_exact)).astype(jnp.int32)
    large = jnp.minimum(large, REL_BUCKETS - 1)
    return jnp.where(dist < max_exact, dist, large)


def _bias_tables(rel_bias):
    dist = (BLOCK + jnp.arange(BLOCK))[:, None] - jnp.arange(2 * BLOCK)[None, :]
    dist = jnp.maximum(dist, 0)
    buckets = jnp.stack([_t5_bucket(dist * dil) for _, dil in DIL_PATTERNS]).astype(jnp.int32)
    head_patterns = (0,) * SW_Q_HEADS + tuple(g for g in range(N_DIL) for _ in range(DIL_HEADS))
    n_heads = len(head_patterns)
    return pl.pallas_call(
        functools.partial(_bias_kernel, head_patterns=head_patterns),
        in_specs=[pl.BlockSpec(memory_space=pltpu.SMEM), _full(buckets.shape)],
        out_specs=_full((n_heads, BLOCK, 2 * BLOCK)),
        out_shape=jax.ShapeDtypeStruct((n_heads, BLOCK, 2 * BLOCK), F32),
        grid=(1,),
        name="rel_bias",
    )(rel_bias, buckets)


def _merge_kernel(*refs, with_router):
    (x_ref, glow_ref, oa_ref, ob_ref, oc_ref, d0_ref, d1_ref, d2_ref, l0_ref, l1_ref, l2_ref,
     wgb_ref, bg_ref, wbr_ref, wout_ref, g2_ref) = refs[:16]
    rest = refs[16:]
    if with_router:
        wr_hi_ref, wr_lo_ref, br_ref = rest[:3]
        rest = rest[3:]
    x1_ref, h2_ref = rest[:2]

    l0, l1, l2 = l0_ref[...], l1_ref[...], l2_ref[...]
    mx = jnp.maximum(jnp.maximum(l0, l1), l2)
    e0, e1, e2 = jnp.exp(l0 - mx), jnp.exp(l1 - mx), jnp.exp(l2 - mx)
    od = (e0 * d0_ref[...] + e1 * d1_ref[...] + e2 * d2_ref[...]) / (e0 + e1 + e2)

    glow = glow_ref[...]
    branches = (oa_ref[...], ob_ref[...], oc_ref[...], od.astype(BF16))
    y = None
    row = 0
    for i, o in enumerate(branches):
        cols = slice(i * D_MODEL, (i + 1) * D_MODEL)
        pre = _dot(glow, wgb_ref[:, cols]) + bg_ref[:, cols]
        gate = 0.5 * jnp.tanh(0.5 * pre) + 0.5
        term = gate * _dot(o, wbr_ref[row:row + BR_SIZES[i], :])
        y = term if y is None else y + term
        row += BR_SIZES[i]
    x1 = x_ref[...] + _dot(y.astype(BF16), wout_ref[...])
    x1_ref[...] = x1
    h2 = _rms(x1, g2_ref[...])
    h2_ref[...] = h2.astype(BF16)

    if with_router:
        wts_ref = rest[2]
        hi, lo = _split_bf16(h2)
        whi = wr_hi_ref[...]
        lg = _dot(hi, whi) + _dot(lo, whi) + _dot(hi, wr_lo_ref[...]) + br_ref[...]
        lane = lax.broadcasted_iota(jnp.int32, lg.shape, 1).astype(F32)
        big = float(LANES)
        m1 = jnp.max(lg, axis=-1, keepdims=True)
        i1 = jnp.min(jnp.where(lg == m1, lane, big), axis=-1, keepdims=True)
        lg2 = jnp.where(lane == i1, NEG, lg)
        m2 = jnp.max(lg2, axis=-1, keepdims=True)
        i2 = jnp.min(jnp.where(lg2 == m2, lane, big), axis=-1, keepdims=True)
        e = jnp.exp(m2 - m1)
        w1 = 1.0 / (1.0 + e)
        w2 = e / (1.0 + e)
        wts_ref[...] = jnp.where(lane == i1, w1, 0.0) + jnp.where(lane == i2, w2, 0.0)


def _merge(x2, glow, oa, ob, oc, dil_o, dil_l, wgb, bg, wbr, wout, g2, router):
    n = x2.shape[0]
    tm = ROW_TILE
    row = lambda w: pl.BlockSpec((tm, w), lambda i: (i, 0))
    acts = [x2, glow, oa, ob, oc, *dil_o, *dil_l]
    weights = [wgb, bg, wbr, wout, g2] + (list(router) if router is not None else [])
    out_shape = [jax.ShapeDtypeStruct((n, D_MODEL), F32), jax.ShapeDtypeStruct((n, D_MODEL), BF16)]
    out_specs = [row(D_MODEL), row(D_MODEL)]
    if router is not None:
        out_shape.append(jax.ShapeDtypeStruct((n, LANES), F32))
        out_specs.append(row(LANES))
    return pl.pallas_call(
        functools.partial(_merge_kernel, with_router=router is not None),
        grid=(n // tm,),
        in_specs=[row(a.shape[1]) for a in acts] + [_full(w.shape) for w in weights],
        out_specs=out_specs,
        out_shape=out_shape,
        compiler_params=_params(("parallel",)),
        name="merge",
    )(*acts, *weights)


FF_CHUNK = 512


def _dense_ffn_kernel(x1_ref, h2_ref, wg_ref, wu_ref, wd_ref, o_ref):
    h2 = h2_ref[...]
    acc = x1_ref[...]
    for c in range(D_FF_DENSE // FF_CHUNK):
        cols = slice(c * FF_CHUNK, (c + 1) * FF_CHUNK)
        g = _dot(h2, wg_ref[:, cols])
        u = _dot(h2, wu_ref[:, cols])
        act = (g / (1.0 + jnp.exp(-g)) * u).astype(BF16)
        acc = acc + _dot(act, wd_ref[cols, :])
    o_ref[...] = acc


def _dense_ffn(x1, h2, wg, wu, wd):
    n = x1.shape[0]
    tm = ROW_TILE
    row = pl.BlockSpec((tm, D_MODEL), lambda i: (i, 0))
    return pl.pallas_call(
        _dense_ffn_kernel,
        grid=(n // tm,),
        in_specs=[row, row, _full(wg.shape), _full(wu.shape), _full(wd.shape)],
        out_specs=row,
        out_shape=jax.ShapeDtypeStruct((n, D_MODEL), F32),
        compiler_params=_params(("parallel",)),
        name="dense_ffn",
    )(x1, h2, wg, wu, wd)


def _moe_kernel(x1_ref, h2_ref, wts_ref, wg_ref, wu_ref, wd_ref, o_ref, acc_ref):
    e = pl.program_id(1)

    @pl.when(e == 0)
    def _():
        acc_ref[...] = x1_ref[...]

    wts = wts_ref[...]
    lane = lax.broadcasted_iota(jnp.int32, wts.shape, 1)
    w_e = jnp.sum(jnp.where(lane == e, wts, 0.0), axis=-1, keepdims=True)
    h2 = h2_ref[...]
    g = _dot(h2, wg_ref[0])
    u = _dot(h2, wu_ref[0])
    act = (g / (1.0 + jnp.exp(-g)) * u).astype(BF16)
    acc_ref[...] += w_e * _dot(act, wd_ref[0])

    @pl.when(e == N_EXPERTS - 1)
    def _():
        o_ref[...] = acc_ref[...]


def _moe_ffn(x1, h2, wts, wg, wu, wd):
    n = x1.shape[0]
    tm = ROW_TILE
    row = lambda w: pl.BlockSpec((tm, w), lambda i, e: (i, 0))
    return pl.pallas_call(
        _moe_kernel,
        grid=(n // tm, N_EXPERTS),
        in_specs=[row(D_MODEL), row(D_MODEL), row(LANES),
                  pl.BlockSpec((1, D_MODEL, D_FF_EXPERT), lambda i, e: (e, 0, 0)),
                  pl.BlockSpec((1, D_MODEL, D_FF_EXPERT), lambda i, e: (e, 0, 0)),
                  pl.BlockSpec((1, D_FF_EXPERT, D_MODEL), lambda i, e: (e, 0, 0))],
        out_specs=row(D_MODEL),
        out_shape=jax.ShapeDtypeStruct((n, D_MODEL), F32),
        scratch_shapes=[pltpu.VMEM((tm, D_MODEL), F32)],
        compiler_params=_params(("parallel", "arbitrary")),
        name="moe_ffn",
    )(x1, h2, wts, wg, wu, wd)


def _rope_tables(seq):
    half = MLA_ROPE // 2
    inv = ROPE_THETA ** (-jnp.arange(half, dtype=F32) / half)
    ang = jnp.arange(seq, dtype=F32)[:, None] * inv[None, :]
    cos, sin = jnp.cos(ang), jnp.sin(ang)
    zeros = lambda w: jnp.zeros((seq, w), F32)
    tail = LANES - MLA_QK
    rc = jnp.concatenate([jnp.ones((seq, MLA_NOPE), F32), cos, cos, zeros(tail)], axis=1)
    rs1 = jnp.concatenate([zeros(MLA_NOPE + half), sin, zeros(tail)], axis=1)
    rs2 = jnp.concatenate([zeros(MLA_NOPE), -sin, zeros(half + tail)], axis=1)
    return rc, rs1, rs2


def _pad_heads(w, heads, dim):
    rows = w.shape[0]
    w = w.reshape(rows, heads, dim)
    return jnp.pad(w, ((0, 0), (0, 0), (0, LANES - dim))).reshape(rows, heads * LANES)


def _layer_weights(w_in, g_qa, w_qb, g_kva, w_kvb, qk_g_mla, qk_g_sw, qk_g_dil, w_gate_a):
    offs = np.concatenate([[0], np.cumsum(IN_SIZES)]).tolist()
    cols = [w_in[:, offs[j]:offs[j + 1]] for j in range(len(IN_SIZES))]
    a_q, a_k, a_v, b_cq, b_ckv, b_kpe, c_q, c_k, c_v, d_q, d_k, d_v = cols
    wa = jnp.concatenate([a_q * (SB_DIM ** -0.5), a_k, a_v], axis=1)
    kpe_pad = jnp.pad(b_kpe, ((0, 0), (MLA_NOPE, LANES - MLA_QK)))
    wb = jnp.concatenate([b_cq, b_ckv, kpe_pad], axis=1)
    rep = SW_Q_HEADS // SW_KV_HEADS
    expand = lambda w: jnp.repeat(w.reshape(D_MODEL, SW_KV_HEADS, SW_DIM), rep, axis=1).reshape(D_MODEL, SW_W)
    wc = jnp.concatenate([c_q, expand(c_k), expand(c_v)], axis=1)
    wd = jnp.concatenate([d_q, d_k, d_v], axis=1)
    wqb = _pad_heads(w_qb, MLA_HEADS, MLA_QK)
    kvb = w_kvb.reshape(MLA_KV_RANK, MLA_HEADS, MLA_NOPE + MLA_V)
    wkvk = _pad_heads(kvb[:, :, :MLA_NOPE].reshape(MLA_KV_RANK, -1), MLA_HEADS, MLA_NOPE)
    wkvv = _pad_heads(kvb[:, :, MLA_NOPE:].reshape(MLA_KV_RANK, -1), MLA_HEADS, MLA_V)
    pad_gain = lambda g: jnp.tile(jnp.pad(g, (0, LANES - MLA_QK)), MLA_HEADS)[None, :]
    gmq = pad_gain(qk_g_mla[0]) * (MLA_QK ** -0.5 * math.log2(math.e))
    gmk = pad_gain(qk_g_mla[1])
    bound = jnp.max(jnp.abs(qk_g_mla[0] * qk_g_mla[1])) * (MLA_QK ** 0.5 * math.log2(math.e))
    shift = (bound * MLA_BOUND_MARGIN).astype(BF16).astype(F32)
    bounded = shift < MLA_MAX_SHIFT
    spare = lambda lo, hi: jnp.tile(jnp.pad(jnp.ones((hi - lo,), F32), (lo, LANES - hi)), MLA_HEADS)[None, :]
    oq = jnp.where(bounded, -shift, 0.0) * spare(MLA_QK, MLA_QK + 1)
    ok = spare(MLA_QK, MLA_QK + 1)
    ov = spare(MLA_V, LANES)
    gsq = jnp.tile(qk_g_sw[0], SW_Q_HEADS)[None, :] * (SW_DIM ** -0.5 * math.log2(math.e))
    gsk = jnp.tile(qk_g_sw[1], SW_Q_HEADS)[None, :]
    gdq = jnp.tile(qk_g_dil[0], N_DIL * DIL_HEADS)[None, :] * (DIL_DIM ** -0.5 * math.log2(math.e))
    gdk = jnp.tile(qk_g_dil[1], N_DIL * DIL_HEADS)[None, :]
    bf = lambda w: w.astype(BF16)
    return (bf(wa), bf(wb), bf(wc), bf(wd), bf(w_gate_a), g_qa[None, :], bf(wqb), g_kva[None, :],
            bf(wkvk), bf(wkvv)), (gmq, gmk, gsq, gsk, gdq, gdk, oq, ok, ov), bounded


def kernel(x, norm1_g, w_in, g_qa, w_qb, g_kva, w_kvb, qk_g_mla, qk_g_sw, qk_g_dil, sinks, rel_bias, w_gate_a, w_gate_b, b_gate, w_branch, w_out, norm2_g, w_gu_dense, w_down_dense, w_router, b_router, w_gu_exp, w_down_exp):
    b, s, d = x.shape
    n = b * s
    depth = norm1_g.shape[0]
    bias = _bias_tables(rel_bias)
    sw_bias = bias[:SW_Q_HEADS]
    dil_bias = [bias[SW_Q_HEADS + g * DIL_HEADS:SW_Q_HEADS + (g + 1) * DIL_HEADS] for g in range(N_DIL)]
    rc, rs1, rs2 = _rope_tables(s)
    x2 = x.reshape(n, d)
    seq3 = lambda a: a.reshape(b, s, a.shape[-1])
    flat = lambda a: a.reshape(n, a.shape[-1])

    for i in range(depth):
        weights, gains, bounded = _layer_weights(w_in[i], g_qa[i], w_qb[i], g_kva[i], w_kvb[i],
                                                 qk_g_mla[i], qk_g_sw[i], qk_g_dil[i], w_gate_a[i])
        wa, wb, wc, wd, wg, gqa, wqb, gkva, wkvk, wkvv = weights
        (sbq, sbk, sbv, mq, mk, mv, swq, swk, swv,
         dq0, dq1, dq2, dk0, dk1, dk2, dv0, dv1, dv2, glow) = _inproj(
            x2, norm1_g[i][None, :], wa, wb, wc, wd, wg, gqa, wqb, gkva, wkvk, wkvv, rc, rs1, rs2,
            *gains, s)

        out_a = _sb_attention(seq3(sbq), seq3(sbk), seq3(sbv))
        out_b = _mla_attention(seq3(mq), seq3(mk), seq3(mv), bounded)
        out_c = _band_attention(seq3(swq), seq3(swk), seq3(swv), sw_bias, sinks[i], 1,
                                SW_Q_HEADS, False, False, BF16)[0]
        dil_o, dil_l = [], []
        for g, (dq, dk, dv) in enumerate(((dq0, dk0, dv0), (dq1, dk1, dv1), (dq2, dk2, dv2))):
            o, l = _band_attention(seq3(dq), seq3(dk), seq3(dv), dil_bias[g], None,
                                   DIL_PATTERNS[g][1], DIL_HEADS, True, True, F32)
            dil_o.append(flat(o))
            dil_l.append(flat(l))

        router = None
        if i % 2 == 1:
            wr = jnp.pad(w_router[i // 2], ((0, 0), (0, LANES - N_EXPERTS)))
            wr_hi = wr.astype(BF16)
            wr_lo = (wr - wr_hi.astype(F32)).astype(BF16)
            br = jnp.pad(b_router[i // 2], (0, LANES - N_EXPERTS), constant_values=NEG)[None, :]
            router = (wr_hi, wr_lo, br)
        merged = _merge(x2, glow, flat(out_a), flat(out_b), flat(out_c), dil_o, dil_l,
                        w_gate_b[i].astype(BF16), b_gate[i][None, :], w_branch[i].astype(BF16),
                        w_out[i].astype(BF16), norm2_g[i][None, :], router)
        if i % 2 == 0:
            x1, h2 = merged
            w_gu = w_gu_dense[i // 2].astype(BF16)
            x2 = _dense_ffn(x1, h2, w_gu[:, :D_FF_DENSE], w_gu[:, D_FF_DENSE:],
                            w_down_dense[i // 2].astype(BF16))
        else:
            x1, h2, wts = merged
            w_gu = w_gu_exp[i // 2].astype(BF16)
            x2 = _moe_ffn(x1, h2, wts, w_gu[:, :, :D_FF_EXPERT], w_gu[:, :, D_FF_EXPERT:],
                          w_down_exp[i // 2].astype(BF16))
    return x2.reshape(b, s, d)
```

```python
import functools
import math

import jax
import jax.numpy as jnp
import numpy as np
from jax import lax
from jax.experimental import pallas as pl
from jax.experimental.pallas import tpu as pltpu

F32 = jnp.float32
BF16 = jnp.bfloat16

D_MODEL = 1024
BLOCK = 128
EPS = 1e-6
SB_HEADS, SB_DIM = 4, 64
MLA_HEADS, MLA_Q_RANK, MLA_KV_RANK = 4, 256, 256
MLA_NOPE, MLA_ROPE, MLA_V = 64, 32, 64
MLA_QK = MLA_NOPE + MLA_ROPE
ROPE_THETA = 10000.0
SW_Q_HEADS, SW_KV_HEADS, SW_DIM = 8, 2, 32
DIL_PATTERNS = ((128, 1), (512, 4), (2048, 16))
DIL_HEADS, DIL_DIM = 4, 32
N_DIL = len(DIL_PATTERNS)
REL_BUCKETS, REL_MAX_DIST = 32, 2048
GATE_RANK, N_BRANCH = 128, 4
D_FF_DENSE = 2048
N_EXPERTS, TOP_K, D_FF_EXPERT = 8, 2, 768

SB_W = SB_HEADS * SB_DIM
SW_W = SW_Q_HEADS * SW_DIM
SW_KV_W = SW_KV_HEADS * SW_DIM
DIL_GW = DIL_HEADS * DIL_DIM
DIL_W = N_DIL * DIL_GW
IN_SIZES = (SB_W, SB_W, SB_W, MLA_Q_RANK, MLA_KV_RANK, MLA_ROPE, SW_W, SW_KV_W, SW_KV_W, DIL_W, DIL_W, DIL_W)
BR_SIZES = (SB_W, MLA_HEADS * MLA_V, SW_W, DIL_GW)

LANES = 128
MLA_PAD = LANES
NEG = -1e30
VMEM_LIMIT = 56 * 1024 * 1024

ROW_TILE = 512
MERGE_ROW_TILE = 1024
MOE_ROW_TILE = 1024
MOE_EXPERTS_PER_STEP = 2
ATT_TILE = 256
MLA_Q_TILE = 2048
MLA_K_TILE = 512
SB_LOG_FLOOR = -152.0
MLA_BOUND_MARGIN = 1.03
MLA_MAX_SHIFT = 60.0
BAND_PLAN = {1: (8, 1), 4: (8, 1), 16: (2, 4)}


def _dot(a, b):
    return jnp.dot(a, b, preferred_element_type=F32)


def _dot_nt(a, b):
    return lax.dot_general(a, b, (((1,), (1,)), ((), ())), preferred_element_type=F32)


def _dot_tn(a, b):
    return lax.dot_general(a, b, (((0,), (0,)), ((), ())), preferred_element_type=F32)


def _params(sem):
    return pltpu.CompilerParams(dimension_semantics=sem, vmem_limit_bytes=VMEM_LIMIT)


def _full(shape):
    nd = len(shape)
    return pl.BlockSpec(shape, lambda *_: (0,) * nd)


def _rms(v, g):
    return v * lax.rsqrt(jnp.mean(v * v, axis=-1, keepdims=True) + EPS) * g


def _split_bf16(v):
    hi = v.astype(BF16)
    lo = (v - hi.astype(F32)).astype(BF16)
    return hi, lo


def _group_matrix(group):
    r = (lax.broadcasted_iota(jnp.int32, (2 * LANES, LANES), 0) & (LANES - 1)) // group
    c = lax.broadcasted_iota(jnp.int32, (2 * LANES, LANES), 1) // group
    return jnp.where(r == c, 1.0, 0.0).astype(BF16)


def _group_norm_slab(xs, gmat, inv_n):
    hi, lo = _split_bf16(xs * xs)
    ss = _dot(jnp.concatenate([hi, lo], axis=1), gmat)
    return xs * lax.rsqrt(ss * inv_n + EPS)


def _inproj_kernel(x_ref, g1_ref, wa_ref, wb_ref, wc_ref, wd_ref, wg_ref,
                   gqa_ref, wqb_ref, gkva_ref, wkvk_ref, wkvv_ref,
                   rc_ref, rs1_ref, rs2_ref,
                   gmq_ref, gmk_ref, gsq_ref, gsk_ref, gdq_ref, gdk_ref, oq_ref, ok_ref, ov_ref,
                   sbq, sbk, sbv, mq, mk, mv, swq, swk, swv,
                   dq0, dq1, dq2, dk0, dk1, dk2, dv0, dv1, dv2, glow):
    x = x_ref[...]
    h = _rms(x, g1_ref[...]).astype(BF16)

    pa = _dot(h, wa_ref[...])
    sbq[...] = (pa[:, :SB_W] * math.log2(math.e)).astype(BF16)
    sbk[...] = pa[:, SB_W:2 * SB_W].astype(BF16)
    sbv[...] = pa[:, 2 * SB_W:].astype(BF16)

    pb = _dot(h, wb_ref[...])
    cq = pb[:, :MLA_Q_RANK]
    ckv = pb[:, MLA_Q_RANK:MLA_Q_RANK + MLA_KV_RANK]
    kpe = pb[:, MLA_Q_RANK + MLA_KV_RANK:]
    half = MLA_ROPE // 2

    def rope(v, reps):
        tile = lambda t: jnp.concatenate([t] * reps, axis=1) if reps > 1 else t
        width = reps * LANES
        return (v * tile(rc_ref[...]) + pltpu.roll(v, half, 1) * tile(rs1_ref[...])
                + pltpu.roll(v, width - half, 1) * tile(rs2_ref[...]))

    ones = _group_matrix(LANES)
    q = rope(_dot(_rms(cq, gqa_ref[...]).astype(BF16), wqb_ref[...]), MLA_HEADS)
    kvn = _rms(ckv, gkva_ref[...]).astype(BF16)
    k = _dot(kvn, wkvk_ref[...]) + jnp.concatenate([rope(kpe, 1)] * MLA_HEADS, axis=1)
    mv[...] = (_dot(kvn, wkvv_ref[...]) + ov_ref[...]).astype(BF16)
    gmq = gmq_ref[...]
    gmk = gmk_ref[...]
    oq = oq_ref[...]
    ok = ok_ref[...]
    for s in range(MLA_HEADS):
        sl = slice(s * LANES, (s + 1) * LANES)
        mq[:, sl] = (_group_norm_slab(q[:, sl], ones, 1.0 / MLA_QK) * gmq[:, sl] + oq[:, sl]).astype(BF16)
        mk[:, sl] = (_group_norm_slab(k[:, sl], ones, 1.0 / MLA_QK) * gmk[:, sl] + ok[:, sl]).astype(BF16)

    g32 = _group_matrix(SW_DIM)
    pc = _dot(h, wc_ref[...])
    gsq = gsq_ref[...]
    gsk = gsk_ref[...]
    for s in range(SW_W // LANES):
        sl = slice(s * LANES, (s + 1) * LANES)
        ks = slice(SW_W + s * LANES, SW_W + (s + 1) * LANES)
        swq[:, sl] = (_group_norm_slab(pc[:, sl], g32, 1.0 / SW_DIM) * gsq[:, sl]).astype(BF16)
        swk[:, sl] = (_group_norm_slab(pc[:, ks], g32, 1.0 / SW_DIM) * gsk[:, sl]).astype(BF16)
    swv[...] = pc[:, 2 * SW_W:].astype(BF16)

    pd = _dot(h, wd_ref[...])
    gdq = gdq_ref[...]
    gdk = gdk_ref[...]
    for g, (q_out, k_out, v_out) in enumerate(((dq0, dk0, dv0), (dq1, dk1, dv1), (dq2, dk2, dv2))):
        sl = slice(g * LANES, (g + 1) * LANES)
        ks = slice(DIL_W + g * LANES, DIL_W + (g + 1) * LANES)
        vs = slice(2 * DIL_W + g * LANES, 2 * DIL_W + (g + 1) * LANES)
        q_out[...] = _group_norm_slab(pd[:, sl], g32, 1.0 / DIL_DIM) * gdq[:, sl]
        k_out[...] = _group_norm_slab(pd[:, ks], g32, 1.0 / DIL_DIM) * gdk[:, sl]
        v_out[...] = pd[:, vs]

    glow[...] = _dot(h, wg_ref[...]).astype(BF16)


def _inproj(x2, g1, wa, wb, wc, wd, wg, gqa, wqb, gkva, wkvk, wkvv, rc, rs1, rs2,
            gmq, gmk, gsq, gsk, gdq, gdk, oq, ok, ov, seq):
    n = x2.shape[0]
    tm = ROW_TILE
    n_seq_tiles = seq // tm
    row = lambda w: pl.BlockSpec((tm, w), lambda i: (i, 0))
    pos = lambda w: pl.BlockSpec((tm, w), lambda i: (i % n_seq_tiles, 0))
    weights = (g1, wa, wb, wc, wd, wg, gqa, wqb, gkva, wkvk, wkvv)
    gains = (gmq, gmk, gsq, gsk, gdq, gdk, oq, ok, ov)
    outs = [(SB_W, BF16)] * 3 + [(MLA_HEADS * MLA_PAD, BF16)] * 3 + [(SW_W, BF16)] * 3 \
        + [(DIL_GW, F32)] * 9 + [(GATE_RANK, BF16)]
    return pl.pallas_call(
        _inproj_kernel,
        grid=(n // tm,),
        in_specs=[row(D_MODEL)] + [_full(w.shape) for w in weights] + [pos(LANES)] * 3
        + [_full(g.shape) for g in gains],
        out_specs=[row(w) for w, _ in outs],
        out_shape=[jax.ShapeDtypeStruct((n, w), dt) for w, dt in outs],
        compiler_params=_params(("parallel",)),
        name="inproj",
    )(x2, *weights, rc, rs1, rs2, *gains)


def _sb_kernel(q_ref, k_ref, v_ref, o_ref, acc_ref, car_ref):
    t = ATT_TILE
    pairs = SB_HEADS // 2
    i = pl.program_id(1)
    lane = lax.broadcasted_iota(jnp.int32, (t, LANES), 1)
    qs = []
    for p in range(pairs):
        q = q_ref[0, :, p * LANES:(p + 1) * LANES]
        zero = jnp.zeros_like(q)
        qs.append(jnp.concatenate([jnp.where(lane < SB_DIM, q, zero), jnp.where(lane >= SB_DIM, q, zero)], axis=0))
    r = lax.broadcasted_iota(jnp.int32, (t, t), 0)
    c = lax.broadcasted_iota(jnp.int32, (t, t), 1)
    suffix = jnp.where(r >= c, 1.0, 0.0).astype(BF16)
    r2 = lax.broadcasted_iota(jnp.int32, (2 * t, t), 0) & (t - 1)
    c2 = lax.broadcasted_iota(jnp.int32, (2 * t, t), 1)
    strict = c2 < r2

    acc_ref[...] = jnp.zeros_like(acc_ref)
    car_ref[...] = jnp.zeros_like(car_ref)

    def block(j, diagonal):
        off = pl.multiple_of(j * t, t)
        kb = k_ref[0, pl.ds(off, t), :]
        vb = v_ref[0, pl.ds(off, t), :]
        zs = [_dot_nt(qs[p], kb[:, p * LANES:(p + 1) * LANES]) for p in range(pairs)]
        lks = []
        for z in zs:
            lk = -(jnp.maximum(z, 0.0) + jnp.log2(1.0 + jnp.exp2(-jnp.abs(z))))
            lks.append(jnp.where(strict, lk, 0.0) if diagonal else lk)
        withins = [_dot(lk.astype(BF16), suffix) for lk in lks]
        weights, cars = [], []
        for p in range(pairs):
            car = car_ref[p]
            a = jnp.exp2(jnp.minimum(zs[p] + withins[p], 0.0) + car)
            weights.append((jnp.where(strict, a, 0.0) if diagonal else a).astype(BF16))
            cars.append(car + withins[p][:, :1])
            car_ref[p] = cars[p]
        for p in range(pairs):
            acc_ref[p] += _dot(weights[p], vb[:, p * LANES:(p + 1) * LANES])
        return jnp.max(functools.reduce(jnp.maximum, cars)) > SB_LOG_FLOOR

    def live(state):
        s, alive = state
        return jnp.logical_and(s < i, alive)

    def body(state):
        s, _ = state
        return s + 1, block(i - 1 - s, False)

    lax.while_loop(live, body, (0, block(i, True)))
    for p in range(pairs):
        o_ref[0, :, p * LANES:(p + 1) * LANES] = jnp.where(
            lane < SB_DIM, acc_ref[p, :t], acc_ref[p, t:]).astype(o_ref.dtype)


def _sb_attention(q, k, v):
    b, s, _ = q.shape
    t = ATT_TILE
    pairs = SB_HEADS // 2
    return pl.pallas_call(
        _sb_kernel,
        grid=(b, s // t),
        in_specs=[pl.BlockSpec((1, t, SB_W), lambda bi, i: (bi, i, 0)),
                  pl.BlockSpec((1, s, SB_W), lambda bi, i: (bi, 0, 0)),
                  pl.BlockSpec((1, s, SB_W), lambda bi, i: (bi, 0, 0))],
        out_specs=pl.BlockSpec((1, t, SB_W), lambda bi, i: (bi, i, 0)),
        out_shape=jax.ShapeDtypeStruct((b, s, SB_W), BF16),
        scratch_shapes=[pltpu.VMEM((pairs, 2 * t, LANES), F32), pltpu.VMEM((pairs, 2 * t, 1), F32)],
        compiler_params=_params(("parallel", "arbitrary")),
        name="stick_breaking",
    )(q, k, v)


def _mla_kernel(q_ref, k_ref, v_ref, o_ref, acc_ref, *stat_refs, online):
    tq, tk = MLA_Q_TILE, MLA_K_TILE
    sub = tq // tk
    i = pl.program_id(2)
    r = lax.broadcasted_iota(jnp.int32, (tk, tk), 0)
    c = lax.broadcasted_iota(jnp.int32, (tk, tk), 1)
    causal = c <= r
    lane = lax.broadcasted_iota(jnp.int32, (tq, LANES), 1)
    reps = tk // LANES

    acc_ref[...] = jnp.zeros_like(acc_ref)
    if online:
        m_ref = stat_refs[0]
        m_ref[...] = jnp.full_like(m_ref, NEG)

    def block(j, rows, diagonal):
        off = pl.multiple_of(j * tk, tk)
        kb = k_ref[0, pl.ds(off, tk), :]
        vb = v_ref[0, pl.ds(off, tk), :]
        for h in range(2):
            sl = slice(h * MLA_PAD, (h + 1) * MLA_PAD)
            s = _dot_nt(q_ref[0, rows, sl], kb[:, sl])
            if diagonal:
                s = jnp.where(causal, s, NEG)
            if online:
                m_old = m_ref[h, rows, :]
                m_new = jnp.maximum(m_old, jnp.max(s, axis=-1, keepdims=True))
                p = jnp.exp2(s - jnp.tile(m_new, (1, reps)))
                acc_ref[h, rows, :] = (jnp.exp2(m_old - m_new) * acc_ref[h, rows, :]
                                       + _dot(p.astype(BF16), vb[:, sl]))
                m_ref[h, rows, :] = m_new
            else:
                acc_ref[h, rows, :] += _dot(jnp.exp2(s).astype(BF16), vb[:, sl])

    for a in range(sub):
        rows = slice(a * tk, (a + 1) * tk)
        for d in range(a + 1):
            block(i * sub + d, rows, d == a)

    def body(j, carry):
        block(j, slice(0, tq), False)
        return carry

    lax.fori_loop(0, i * sub, body, 0)
    a0, a1 = acc_ref[0], acc_ref[1]
    half = LANES // 2
    o0 = a0 / pltpu.roll(a0, half, 1)
    o1 = pltpu.roll(a1, half, 1) / a1
    o_ref[0] = jnp.where(lane < MLA_V, o0, o1).astype(o_ref.dtype)


def _mla_attention(q, k, v, bounded):
    b, s, _ = q.shape
    t = MLA_Q_TILE
    stat = pltpu.VMEM((2, t, LANES), F32)

    def call(online):
        return pl.pallas_call(
            functools.partial(_mla_kernel, online=online),
            grid=(b, MLA_HEADS // 2, s // t),
            in_specs=[pl.BlockSpec((1, t, 2 * MLA_PAD), lambda bi, p, i: (bi, i, p)),
                      pl.BlockSpec((1, s, 2 * MLA_PAD), lambda bi, p, i: (bi, 0, p)),
                      pl.BlockSpec((1, s, 2 * MLA_PAD), lambda bi, p, i: (bi, 0, p))],
            out_specs=pl.BlockSpec((1, t, LANES), lambda bi, p, i: (bi, i, p)),
            out_shape=jax.ShapeDtypeStruct((b, s, MLA_HEADS * MLA_V), BF16),
            scratch_shapes=[stat, stat] if online else [stat],
            compiler_params=_params(("parallel", "parallel", "arbitrary")),
            name="latent_attention_online" if online else "latent_attention",
        )

    return lax.cond(bounded, call(False), call(True), q, k, v)


def _band_kernel(*refs, heads, dil, blocks, fan, with_sinks, with_lse):
    q_ref, kp_ref, kc_ref, vp_ref, vc_ref, bias_ref, first_ref = refs[:7]
    rest = refs[7:]
    if with_sinks:
        sink_ref, rest = rest[0], rest[1:]
    o_ref = rest[0]
    width = heads * DIL_DIM
    n = pl.program_id(1)
    group = lax.broadcasted_iota(jnp.int32, (BLOCK, width), 1) // DIL_DIM
    bias = bias_ref[...]
    first_bias = jnp.where(n > 0, bias, first_ref[...])

    def rows(r, blk):
        start = r + blk * (BLOCK * dil)
        return pl.ds(start, BLOCK, stride=dil) if dil > 1 else pl.ds(start, BLOCK)

    def load(ref, r, blk):
        return ref[0, rows(r, blk), :].astype(BF16)

    def streams(r0):
        idx, vs, ss, ps, ls, ms = [], [], [], [], [], []
        for r, j in [(r0 + dr, j) for dr in range(fan) for j in range(blocks)]:
            q = load(q_ref, r, j)
            zero = jnp.zeros_like(q)
            qs = jnp.concatenate([jnp.where(group == h, q, zero) for h in range(heads)], axis=0)
            k_prev = load(kp_ref, r, 0) if j == 0 else load(kc_ref, r, j - 1)
            v_prev = load(vp_ref, r, 0) if j == 0 else load(vc_ref, r, j - 1)
            k = jnp.concatenate([k_prev, load(kc_ref, r, j)], axis=0)
            vs.append(jnp.concatenate([v_prev, load(vc_ref, r, j)], axis=0))
            ss.append(_dot_nt(k, qs) + (first_bias if j == 0 else bias))
            idx.append(rows(r, j))
        for s in ss:
            m = jnp.max(s, axis=0, keepdims=True)
            if with_sinks:
                sink = sink_ref[...]
                m = jnp.maximum(m, sink)
            p = jnp.exp2(s - m)
            l = jnp.sum(p, axis=0, keepdims=True)
            if with_sinks:
                l = l + jnp.exp2(sink - m)
            ps.append(p.astype(BF16))
            ls.append(l)
            ms.append(m)
        os_ = [_dot_tn(v, p) * (1.0 / l) for v, p, l in zip(vs, ps, ls)]
        for j, o in enumerate(os_):
            out = jnp.concatenate([o[h * DIL_DIM:(h + 1) * DIL_DIM, h * BLOCK:(h + 1) * BLOCK]
                                   for h in range(heads)], axis=0)
            o_ref[0, idx[j], :] = out.T.astype(o_ref.dtype)
            if with_lse:
                lg = (ms[j] + jnp.log2(ls[j])) * math.log(2.0)
                lse = jnp.concatenate([jnp.broadcast_to(lg[:, h * BLOCK:(h + 1) * BLOCK], (DIL_DIM, BLOCK))
                                       for h in range(heads)], axis=0)
                rest[1][0, idx[j], :] = lse.T

    if dil == fan:
        streams(0)
    else:
        def body(t, carry):
            streams(t * fan)
            return carry
        lax.fori_loop(0, dil // fan, body, 0)


def _band_attention(q, k, v, bias, sinks, dil, heads, inclusive, with_lse, out_dtype):
    b, s, width = q.shape
    blocks, fan = BAND_PLAN[dil]
    step = blocks * BLOCK * dil
    qi = np.arange(BLOCK)[None, :, None]
    kj = np.arange(2 * BLOCK)[None, None, :]
    window = ((kj >= qi) if inclusive else (kj > qi)) & (kj <= qi + BLOCK)
    table = lambda mask: jnp.where(mask, bias * math.log2(math.e), NEG).transpose(2, 0, 1).reshape(
        2 * BLOCK, heads * BLOCK)
    tables = (table(window), table(window & (kj >= BLOCK)))
    cur = pl.BlockSpec((1, step, width), lambda bi, n: (bi, n, 0))
    prev = pl.BlockSpec((1, BLOCK * dil, width), lambda bi, n: (bi, jnp.maximum(n * blocks - 1, 0), 0))
    in_specs = [cur, prev, cur, prev, cur] + [_full(t.shape) for t in tables]
    args = [q, k, k, v, v, *tables]
    if sinks is not None:
        sink_cols = jnp.repeat(sinks * math.log2(math.e), BLOCK)[None, :]
        in_specs.append(_full(sink_cols.shape))
        args.append(sink_cols)
    out_shape = [jax.ShapeDtypeStruct((b, s, width), out_dtype)]
    out_specs = [cur]
    if with_lse:
        out_shape.append(jax.ShapeDtypeStruct((b, s, width), F32))
        out_specs.append(cur)
    return pl.pallas_call(
        functools.partial(_band_kernel, heads=heads, dil=dil, blocks=blocks, fan=fan,
                          with_sinks=sinks is not None, with_lse=with_lse),
        grid=(b, s // step),
        in_specs=in_specs,
        out_specs=out_specs,
        out_shape=out_shape,
        compiler_params=_params(("parallel", "arbitrary")),
        name="band_attention",
    )(*args)


def _bias_kernel(rb_ref, bkt_ref, o_ref, *, head_patterns):
    for head, pat in enumerate(head_patterns):
        bkt = bkt_ref[pat]
        acc = jnp.zeros(bkt.shape, F32)
        for bucket in range(REL_BUCKETS):
            acc = jnp.where(bkt == bucket, rb_ref[bucket, head], acc)
        o_ref[head] = acc


def _t5_bucket(dist):
    max_exact = REL_BUCKETS // 2
    d = jnp.maximum(dist, 1).astype(F32)
    large = max_exact + (jnp.log(d / max_exact) / math.log(REL_MAX_DIST / max_exact)
                         * (REL_BUCKETS - max_exact)).astype(jnp.int32)
    large = jnp.minimum(large, REL_BUCKETS - 1)
    return jnp.where(dist < max_exact, dist, large)


def _bias_tables(rel_bias):
    dist = (BLOCK + jnp.arange(BLOCK))[:, None] - jnp.arange(2 * BLOCK)[None, :]
    dist = jnp.maximum(dist, 0)
    buckets = jnp.stack([_t5_bucket(dist * dil) for _, dil in DIL_PATTERNS]).astype(jnp.int32)
    head_patterns = (0,) * SW_Q_HEADS + tuple(g for g in range(N_DIL) for _ in range(DIL_HEADS))
    n_heads = len(head_patterns)
    return pl.pallas_call(
        functools.partial(_bias_kernel, head_patterns=head_patterns),
        in_specs=[pl.BlockSpec(memory_space=pltpu.SMEM), _full(buckets.shape)],
        out_specs=_full((n_heads, BLOCK, 2 * BLOCK)),
        out_shape=jax.ShapeDtypeStruct((n_heads, BLOCK, 2 * BLOCK), F32),
        grid=(1,),
        name="rel_bias",
    )(rel_bias, buckets)


def _merge_kernel(*refs, with_router):
    (x_ref, glow_ref, oa_ref, ob_ref, oc_ref, d0_ref, d1_ref, d2_ref, l0_ref, l1_ref, l2_ref,
     wgb_ref, bg_ref, wbr_ref, wout_ref, g2_ref) = refs[:16]
    rest = refs[16:]
    if with_router:
        wr_hi_ref, wr_lo_ref, br_ref = rest[:3]
        rest = rest[3:]
    x1_ref, h2_ref = rest[:2]

    l0, l1, l2 = l0_ref[...], l1_ref[...], l2_ref[...]
    mx = jnp.maximum(jnp.maximum(l0, l1), l2)
    e0, e1, e2 = jnp.exp(l0 - mx), jnp.exp(l1 - mx), jnp.exp(l2 - mx)
    od = (e0 * d0_ref[...] + e1 * d1_ref[...] + e2 * d2_ref[...]) / (e0 + e1 + e2)

    glow = glow_ref[...]
    branches = (oa_ref[...], ob_ref[...], oc_ref[...], od.astype(BF16))
    y = None
    row = 0
    for i, o in enumerate(branches):
        cols = slice(i * D_MODEL, (i + 1) * D_MODEL)
        pre = _dot(glow, wgb_ref[:, cols]) + bg_ref[:, cols]
        gate = 0.5 * jnp.tanh(0.5 * pre) + 0.5
        term = gate * _dot(o, wbr_ref[row:row + BR_SIZES[i], :])
        y = term if y is None else y + term
        row += BR_SIZES[i]
    x1 = x_ref[...] + _dot(y.astype(BF16), wout_ref[...])
    x1_ref[...] = x1
    h2 = _rms(x1, g2_ref[...])
    h2_ref[...] = h2.astype(BF16)

    if with_router:
        wts_ref = rest[2]
        hi, lo = _split_bf16(h2)
        whi = wr_hi_ref[...]
        lg = _dot(hi, whi) + _dot(lo, whi) + _dot(hi, wr_lo_ref[...]) + br_ref[...]
        lane = lax.broadcasted_iota(jnp.int32, lg.shape, 1).astype(F32)
        big = float(LANES)
        m1 = jnp.max(lg, axis=-1, keepdims=True)
        i1 = jnp.min(jnp.where(lg == m1, lane, big), axis=-1, keepdims=True)
        lg2 = jnp.where(lane == i1, NEG, lg)
        m2 = jnp.max(lg2, axis=-1, keepdims=True)
        i2 = jnp.min(jnp.where(lg2 == m2, lane, big), axis=-1, keepdims=True)
        e = jnp.exp(m2 - m1)
        w1 = 1.0 / (1.0 + e)
        w2 = e / (1.0 + e)
        wts_ref[...] = jnp.where(lane == i1, w1, 0.0) + jnp.where(lane == i2, w2, 0.0)


def _merge(x2, glow, oa, ob, oc, dil_o, dil_l, wgb, bg, wbr, wout, g2, router):
    n = x2.shape[0]
    tm = MERGE_ROW_TILE
    row = lambda w: pl.BlockSpec((tm, w), lambda i: (i, 0))
    acts = [x2, glow, oa, ob, oc, *dil_o, *dil_l]
    weights = [wgb, bg, wbr, wout, g2] + (list(router) if router is not None else [])
    out_shape = [jax.ShapeDtypeStruct((n, D_MODEL), F32), jax.ShapeDtypeStruct((n, D_MODEL), BF16)]
    out_specs = [row(D_MODEL), row(D_MODEL)]
    if router is not None:
        out_shape.append(jax.ShapeDtypeStruct((n, LANES), F32))
        out_specs.append(row(LANES))
    return pl.pallas_call(
        functools.partial(_merge_kernel, with_router=router is not None),
        grid=(n // tm,),
        in_specs=[row(a.shape[1]) for a in acts] + [_full(w.shape) for w in weights],
        out_specs=out_specs,
        out_shape=out_shape,
        compiler_params=_params(("parallel",)),
        name="merge",
    )(*acts, *weights)


FF_CHUNK = 512


def _dense_ffn_kernel(x1_ref, h2_ref, wg_ref, wu_ref, wd_ref, o_ref):
    h2 = h2_ref[...]
    acc = x1_ref[...]
    for c in range(D_FF_DENSE // FF_CHUNK):
        cols = slice(c * FF_CHUNK, (c + 1) * FF_CHUNK)
        g = _dot(h2, wg_ref[:, cols])
        u = _dot(h2, wu_ref[:, cols])
        act = (g / (1.0 + jnp.exp(-g)) * u).astype(BF16)
        acc = acc + _dot(act, wd_ref[cols, :])
    o_ref[...] = acc


def _dense_ffn(x1, h2, wg, wu, wd):
    n = x1.shape[0]
    tm = ROW_TILE
    row = pl.BlockSpec((tm, D_MODEL), lambda i: (i, 0))
    return pl.pallas_call(
        _dense_ffn_kernel,
        grid=(n // tm,),
        in_specs=[row, row, _full(wg.shape), _full(wu.shape), _full(wd.shape)],
        out_specs=row,
        out_shape=jax.ShapeDtypeStruct((n, D_MODEL), F32),
        compiler_params=_params(("parallel",)),
        name="dense_ffn",
    )(x1, h2, wg, wu, wd)


def _moe_kernel(x1_ref, h2_ref, wts_ref, wg_ref, wu_ref, wd_ref, o_ref):
    step = pl.program_id(1)

    @pl.when(step == 0)
    def _():
        o_ref[...] = x1_ref[...]

    wts = wts_ref[...]
    lane = lax.broadcasted_iota(jnp.int32, wts.shape, 1)
    h2 = h2_ref[...]
    y = None
    for j in range(MOE_EXPERTS_PER_STEP):
        e = step * MOE_EXPERTS_PER_STEP + j
        w_e = jnp.sum(jnp.where(lane == e, wts, 0.0), axis=-1, keepdims=True)
        g = _dot(h2, wg_ref[j])
        u = _dot(h2, wu_ref[j])
        act = (g / (1.0 + jnp.exp(-g)) * u).astype(BF16)
        part = w_e * _dot(act, wd_ref[j])
        y = part if y is None else y + part
    o_ref[...] += y


def _moe_ffn(x1, h2, wts, wg, wu, wd):
    n = x1.shape[0]
    tm = MOE_ROW_TILE
    ne = MOE_EXPERTS_PER_STEP
    row = lambda w: pl.BlockSpec((tm, w), lambda i, e: (i, 0))
    return pl.pallas_call(
        _moe_kernel,
        grid=(n // tm, N_EXPERTS // ne),
        in_specs=[row(D_MODEL), row(D_MODEL), row(LANES),
                  pl.BlockSpec((ne, D_MODEL, D_FF_EXPERT), lambda i, e: (e, 0, 0)),
                  pl.BlockSpec((ne, D_MODEL, D_FF_EXPERT), lambda i, e: (e, 0, 0)),
                  pl.BlockSpec((ne, D_FF_EXPERT, D_MODEL), lambda i, e: (e, 0, 0))],
        out_specs=row(D_MODEL),
        out_shape=jax.ShapeDtypeStruct((n, D_MODEL), F32),
        compiler_params=_params(("parallel", "arbitrary")),
        name="moe_ffn",
    )(x1, h2, wts, wg, wu, wd)


def _rope_tables(seq):
    half = MLA_ROPE // 2
    inv = ROPE_THETA ** (-jnp.arange(half, dtype=F32) / half)
    ang = jnp.arange(seq, dtype=F32)[:, None] * inv[None, :]
    cos, sin = jnp.cos(ang), jnp.sin(ang)
    zeros = lambda w: jnp.zeros((seq, w), F32)
    tail = LANES - MLA_QK
    rc = jnp.concatenate([jnp.ones((seq, MLA_NOPE), F32), cos, cos, zeros(tail)], axis=1)
    rs1 = jnp.concatenate([zeros(MLA_NOPE + half), sin, zeros(tail)], axis=1)
    rs2 = jnp.concatenate([zeros(MLA_NOPE), -sin, zeros(half + tail)], axis=1)
    return rc, rs1, rs2


def _pad_heads(w, heads, dim):
    rows = w.shape[0]
    w = w.reshape(rows, heads, dim)
    return jnp.pad(w, ((0, 0), (0, 0), (0, LANES - dim))).reshape(rows, heads * LANES)


def _layer_weights(w_in, g_qa, w_qb, g_kva, w_kvb, qk_g_mla, qk_g_sw, qk_g_dil, w_gate_a):
    offs = np.concatenate([[0], np.cumsum(IN_SIZES)]).tolist()
    cols = [w_in[:, offs[j]:offs[j + 1]] for j in range(len(IN_SIZES))]
    a_q, a_k, a_v, b_cq, b_ckv, b_kpe, c_q, c_k, c_v, d_q, d_k, d_v = cols
    wa = jnp.concatenate([a_q * (SB_DIM ** -0.5), a_k, a_v], axis=1)
    kpe_pad = jnp.pad(b_kpe, ((0, 0), (MLA_NOPE, LANES - MLA_QK)))
    wb = jnp.concatenate([b_cq, b_ckv, kpe_pad], axis=1)
    rep = SW_Q_HEADS // SW_KV_HEADS
    expand = lambda w: jnp.repeat(w.reshape(D_MODEL, SW_KV_HEADS, SW_DIM), rep, axis=1).reshape(D_MODEL, SW_W)
    wc = jnp.concatenate([c_q, expand(c_k), expand(c_v)], axis=1)
    wd = jnp.concatenate([d_q, d_k, d_v], axis=1)
    wqb = _pad_heads(w_qb, MLA_HEADS, MLA_QK)
    kvb = w_kvb.reshape(MLA_KV_RANK, MLA_HEADS, MLA_NOPE + MLA_V)
    wkvk = _pad_heads(kvb[:, :, :MLA_NOPE].reshape(MLA_KV_RANK, -1), MLA_HEADS, MLA_NOPE)
    wkvv = _pad_heads(kvb[:, :, MLA_NOPE:].reshape(MLA_KV_RANK, -1), MLA_HEADS, MLA_V)
    pad_gain = lambda g: jnp.tile(jnp.pad(g, (0, LANES - MLA_QK)), MLA_HEADS)[None, :]
    gmq = pad_gain(qk_g_mla[0]) * (MLA_QK ** -0.5 * math.log2(math.e))
    gmk = pad_gain(qk_g_mla[1])
    bound = jnp.max(jnp.abs(qk_g_mla[0] * qk_g_mla[1])) * (MLA_QK ** 0.5 * math.log2(math.e))
    shift = (bound * MLA_BOUND_MARGIN).astype(BF16).astype(F32)
    bounded = shift < MLA_MAX_SHIFT
    spare = lambda lo, hi: jnp.tile(jnp.pad(jnp.ones((hi - lo,), F32), (lo, LANES - hi)), MLA_HEADS)[None, :]
    oq = jnp.where(bounded, -shift, 0.0) * spare(MLA_QK, MLA_QK + 1)
    ok = spare(MLA_QK, MLA_QK + 1)
    ov = spare(MLA_V, LANES)
    gsq = jnp.tile(qk_g_sw[0], SW_Q_HEADS)[None, :] * (SW_DIM ** -0.5 * math.log2(math.e))
    gsk = jnp.tile(qk_g_sw[1], SW_Q_HEADS)[None, :]
    gdq = jnp.tile(qk_g_dil[0], N_DIL * DIL_HEADS)[None, :] * (DIL_DIM ** -0.5 * math.log2(math.e))
    gdk = jnp.tile(qk_g_dil[1], N_DIL * DIL_HEADS)[None, :]
    bf = lambda w: w.astype(BF16)
    return (bf(wa), bf(wb), bf(wc), bf(wd), bf(w_gate_a), g_qa[None, :], bf(wqb), g_kva[None, :],
            bf(wkvk), bf(wkvv)), (gmq, gmk, gsq, gsk, gdq, gdk, oq, ok, ov), bounded


def kernel(x, norm1_g, w_in, g_qa, w_qb, g_kva, w_kvb, qk_g_mla, qk_g_sw, qk_g_dil, sinks, rel_bias, w_gate_a, w_gate_b, b_gate, w_branch, w_out, norm2_g, w_gu_dense, w_down_dense, w_router, b_router, w_gu_exp, w_down_exp):
    b, s, d = x.shape
    n = b * s
    depth = norm1_g.shape[0]
    bias = _bias_tables(rel_bias)
    sw_bias = bias[:SW_Q_HEADS]
    dil_bias = [bias[SW_Q_HEADS + g * DIL_HEADS:SW_Q_HEADS + (g + 1) * DIL_HEADS] for g in range(N_DIL)]
    rc, rs1, rs2 = _rope_tables(s)
    x2 = x.reshape(n, d)
    seq3 = lambda a: a.reshape(b, s, a.shape[-1])
    flat = lambda a: a.reshape(n, a.shape[-1])

    for i in range(depth):
        weights, gains, bounded = _layer_weights(w_in[i], g_qa[i], w_qb[i], g_kva[i], w_kvb[i],
                                                 qk_g_mla[i], qk_g_sw[i], qk_g_dil[i], w_gate_a[i])
        wa, wb, wc, wd, wg, gqa, wqb, gkva, wkvk, wkvv = weights
        (sbq, sbk, sbv, mq, mk, mv, swq, swk, swv,
         dq0, dq1, dq2, dk0, dk1, dk2, dv0, dv1, dv2, glow) = _inproj(
            x2, norm1_g[i][None, :], wa, wb, wc, wd, wg, gqa, wqb, gkva, wkvk, wkvv, rc, rs1, rs2,
            *gains, s)

        out_a = _sb_attention(seq3(sbq), seq3(sbk), seq3(sbv))
        out_b = _mla_attention(seq3(mq), seq3(mk), seq3(mv), bounded)
        out_c = _band_attention(seq3(swq), seq3(swk), seq3(swv), sw_bias, sinks[i], 1,
                                SW_Q_HEADS, False, False, BF16)[0]
        dil_o, dil_l = [], []
        for g, (dq, dk, dv) in enumerate(((dq0, dk0, dv0), (dq1, dk1, dv1), (dq2, dk2, dv2))):
            o, l = _band_attention(seq3(dq), seq3(dk), seq3(dv), dil_bias[g], None,
                                   DIL_PATTERNS[g][1], DIL_HEADS, True, True, F32)
            dil_o.append(flat(o))
            dil_l.append(flat(l))

        router = None
        if i % 2 == 1:
            wr = jnp.pad(w_router[i // 2], ((0, 0), (0, LANES - N_EXPERTS)))
            wr_hi = wr.astype(BF16)
            wr_lo = (wr - wr_hi.astype(F32)).astype(BF16)
            br = jnp.pad(b_router[i // 2], (0, LANES - N_EXPERTS), constant_values=NEG)[None, :]
            router = (wr_hi, wr_lo, br)
        merged = _merge(x2, glow, flat(out_a), flat(out_b), flat(out_c), dil_o, dil_l,
                        w_gate_b[i].astype(BF16), b_gate[i][None, :], w_branch[i].astype(BF16),
                        w_out[i].astype(BF16), norm2_g[i][None, :], router)
        if i % 2 == 0:
            x1, h2 = merged
            w_gu = w_gu_dense[i // 2].astype(BF16)
            x2 = _dense_ffn(x1, h2, w_gu[:, :D_FF_DENSE], w_gu[:, D_FF_DENSE:],
                            w_down_dense[i // 2].astype(BF16))
        else:
            x1, h2, wts = merged
            w_gu = w_gu_exp[i // 2].astype(BF16)
            x2 = _moe_ffn(x1, h2, wts, w_gu[:, :, :D_FF_EXPERT], w_gu[:, :, D_FF_EXPERT:],
                          w_down_exp[i // 2].astype(BF16))
    return x2.reshape(b, s, d)
```

```python
import functools
import math

import jax
import jax.numpy as jnp
import numpy as np
from jax import lax
from jax.experimental import pallas as pl
from jax.experimental.pallas import tpu as pltpu

F32 = jnp.float32
BF16 = jnp.bfloat16

D_MODEL = 1024
BLOCK = 128
EPS = 1e-6
SB_HEADS, SB_DIM = 4, 64
MLA_HEADS, MLA_Q_RANK, MLA_KV_RANK = 4, 256, 256
MLA_NOPE, MLA_ROPE, MLA_V = 64, 32, 64
MLA_QK = MLA_NOPE + MLA_ROPE
ROPE_THETA = 10000.0
SW_Q_HEADS, SW_KV_HEADS, SW_DIM = 8, 2, 32
DIL_PATTERNS = ((128, 1), (512, 4), (2048, 16))
DIL_HEADS, DIL_DIM = 4, 32
N_DIL = len(DIL_PATTERNS)
REL_BUCKETS, REL_MAX_DIST = 32, 2048
GATE_RANK, N_BRANCH = 128, 4
D_FF_DENSE = 2048
N_EXPERTS, TOP_K, D_FF_EXPERT = 8, 2, 768

SB_W = SB_HEADS * SB_DIM
SW_W = SW_Q_HEADS * SW_DIM
SW_KV_W = SW_KV_HEADS * SW_DIM
DIL_GW = DIL_HEADS * DIL_DIM
DIL_W = N_DIL * DIL_GW
IN_SIZES = (SB_W, SB_W, SB_W, MLA_Q_RANK, MLA_KV_RANK, MLA_ROPE, SW_W, SW_KV_W, SW_KV_W, DIL_W, DIL_W, DIL_W)
BR_SIZES = (SB_W, MLA_HEADS * MLA_V, SW_W, DIL_GW)

LANES = 128
MLA_PAD = LANES
NEG = -1e30
VMEM_LIMIT = 56 * 1024 * 1024

ROW_TILE = 1024
MERGE_COL_CHUNK = 256
MERGE_ROW_TILE = 1024
MOE_ROW_TILE = 1024
MOE_EXPERTS_PER_STEP = 2
ATT_TILE = 256
MLA_Q_TILE = 2048
MLA_K_TILE = 512
SB_LOG_FLOOR = -152.0
MLA_BOUND_MARGIN = 1.03
MLA_MAX_SHIFT = 60.0
BAND_PLAN = {1: (16, 1), 4: (8, 2), 16: (2, 8)}


def _dot(a, b):
    return jnp.dot(a, b, preferred_element_type=F32)


def _dot_nt(a, b):
    return lax.dot_general(a, b, (((1,), (1,)), ((), ())), preferred_element_type=F32)


def _dot_tn(a, b):
    return lax.dot_general(a, b, (((0,), (0,)), ((), ())), preferred_element_type=F32)


def _params(sem):
    return pltpu.CompilerParams(dimension_semantics=sem, vmem_limit_bytes=VMEM_LIMIT)


def _full(shape):
    nd = len(shape)
    return pl.BlockSpec(shape, lambda *_: (0,) * nd)


def _rms(v, g):
    return v * lax.rsqrt(jnp.mean(v * v, axis=-1, keepdims=True) + EPS) * g


def _split_bf16(v):
    hi = v.astype(BF16)
    lo = (v - hi.astype(F32)).astype(BF16)
    return hi, lo


def _group_matrix(group):
    r = (lax.broadcasted_iota(jnp.int32, (2 * LANES, LANES), 0) & (LANES - 1)) // group
    c = lax.broadcasted_iota(jnp.int32, (2 * LANES, LANES), 1) // group
    return jnp.where(r == c, 1.0, 0.0).astype(BF16)


def _group_norm_slab(xs, gmat, inv_n):
    hi, lo = _split_bf16(xs * xs)
    ss = _dot(jnp.concatenate([hi, lo], axis=1), gmat)
    return xs * lax.rsqrt(ss * inv_n + EPS)


def _inproj_kernel(x_ref, g1_ref, wa_ref, wb_ref, wc_ref, wd_ref, wg_ref,
                   gqa_ref, wqb_ref, gkva_ref, wkvk_ref, wkvv_ref,
                   rc_ref, rs1_ref, rs2_ref,
                   gmq_ref, gmk_ref, gsq_ref, gsk_ref, gdq_ref, gdk_ref, oq_ref, ok_ref, ov_ref,
                   sbq, sbk, sbv, mq, mk, mv, swq, swk, swv,
                   dq0, dq1, dq2, dk0, dk1, dk2, dv0, dv1, dv2, glow):
    x = x_ref[...]
    h = _rms(x, g1_ref[...]).astype(BF16)

    pa = _dot(h, wa_ref[...])
    sbq[...] = (pa[:, :SB_W] * math.log2(math.e)).astype(BF16)
    sbk[...] = pa[:, SB_W:2 * SB_W].astype(BF16)
    sbv[...] = pa[:, 2 * SB_W:].astype(BF16)

    pb = _dot(h, wb_ref[...])
    cq = pb[:, :MLA_Q_RANK]
    ckv = pb[:, MLA_Q_RANK:MLA_Q_RANK + MLA_KV_RANK]
    kpe = pb[:, MLA_Q_RANK + MLA_KV_RANK:]
    half = MLA_ROPE // 2

    def rope(v, reps):
        tile = lambda t: jnp.concatenate([t] * reps, axis=1) if reps > 1 else t
        width = reps * LANES
        return (v * tile(rc_ref[...]) + pltpu.roll(v, half, 1) * tile(rs1_ref[...])
                + pltpu.roll(v, width - half, 1) * tile(rs2_ref[...]))

    ones = _group_matrix(LANES)
    q = rope(_dot(_rms(cq, gqa_ref[...]).astype(BF16), wqb_ref[...]), MLA_HEADS)
    kvn = _rms(ckv, gkva_ref[...]).astype(BF16)
    k = _dot(kvn, wkvk_ref[...]) + jnp.concatenate([rope(kpe, 1)] * MLA_HEADS, axis=1)
    mv[...] = (_dot(kvn, wkvv_ref[...]) + ov_ref[...]).astype(BF16)
    gmq = gmq_ref[...]
    gmk = gmk_ref[...]
    oq = oq_ref[...]
    ok = ok_ref[...]
    for s in range(MLA_HEADS):
        sl = slice(s * LANES, (s + 1) * LANES)
        mq[:, sl] = (_group_norm_slab(q[:, sl], ones, 1.0 / MLA_QK) * gmq[:, sl] + oq[:, sl]).astype(BF16)
        mk[:, sl] = (_group_norm_slab(k[:, sl], ones, 1.0 / MLA_QK) * gmk[:, sl] + ok[:, sl]).astype(BF16)

    g32 = _group_matrix(SW_DIM)
    pc = _dot(h, wc_ref[...])
    gsq = gsq_ref[...]
    gsk = gsk_ref[...]
    for s in range(SW_W // LANES):
        sl = slice(s * LANES, (s + 1) * LANES)
        ks = slice(SW_W + s * LANES, SW_W + (s + 1) * LANES)
        swq[:, sl] = (_group_norm_slab(pc[:, sl], g32, 1.0 / SW_DIM) * gsq[:, sl]).astype(BF16)
        swk[:, sl] = (_group_norm_slab(pc[:, ks], g32, 1.0 / SW_DIM) * gsk[:, sl]).astype(BF16)
    swv[...] = pc[:, 2 * SW_W:].astype(BF16)

    pd = _dot(h, wd_ref[...])
    gdq = gdq_ref[...]
    gdk = gdk_ref[...]
    for g, (q_out, k_out, v_out) in enumerate(((dq0, dk0, dv0), (dq1, dk1, dv1), (dq2, dk2, dv2))):
        sl = slice(g * LANES, (g + 1) * LANES)
        ks = slice(DIL_W + g * LANES, DIL_W + (g + 1) * LANES)
        vs = slice(2 * DIL_W + g * LANES, 2 * DIL_W + (g + 1) * LANES)
        q_out[...] = _group_norm_slab(pd[:, sl], g32, 1.0 / DIL_DIM) * gdq[:, sl]
        k_out[...] = _group_norm_slab(pd[:, ks], g32, 1.0 / DIL_DIM) * gdk[:, sl]
        v_out[...] = pd[:, vs]

    glow[...] = _dot(h, wg_ref[...]).astype(BF16)


def _inproj(x2, g1, wa, wb, wc, wd, wg, gqa, wqb, gkva, wkvk, wkvv, rc, rs1, rs2,
            gmq, gmk, gsq, gsk, gdq, gdk, oq, ok, ov, seq):
    n = x2.shape[0]
    tm = ROW_TILE
    n_seq_tiles = seq // tm
    row = lambda w: pl.BlockSpec((tm, w), lambda i: (i, 0))
    pos = lambda w: pl.BlockSpec((tm, w), lambda i: (i % n_seq_tiles, 0))
    weights = (g1, wa, wb, wc, wd, wg, gqa, wqb, gkva, wkvk, wkvv)
    gains = (gmq, gmk, gsq, gsk, gdq, gdk, oq, ok, ov)
    outs = [(SB_W, BF16)] * 3 + [(MLA_HEADS * MLA_PAD, BF16)] * 3 + [(SW_W, BF16)] * 3 \
        + [(DIL_GW, F32)] * 9 + [(GATE_RANK, BF16)]
    return pl.pallas_call(
        _inproj_kernel,
        grid=(n // tm,),
        in_specs=[row(D_MODEL)] + [_full(w.shape) for w in weights] + [pos(LANES)] * 3
        + [_full(g.shape) for g in gains],
        out_specs=[row(w) for w, _ in outs],
        out_shape=[jax.ShapeDtypeStruct((n, w), dt) for w, dt in outs],
        compiler_params=_params(("parallel",)),
        name="inproj",
    )(x2, *weights, rc, rs1, rs2, *gains)


def _sb_kernel(q_ref, k_ref, v_ref, o_ref, acc_ref, car_ref):
    t = ATT_TILE
    pairs = SB_HEADS // 2
    i = pl.program_id(1)
    lane = lax.broadcasted_iota(jnp.int32, (t, LANES), 1)
    qs = []
    for p in range(pairs):
        q = q_ref[0, :, p * LANES:(p + 1) * LANES]
        zero = jnp.zeros_like(q)
        qs.append(jnp.concatenate([jnp.where(lane < SB_DIM, q, zero), jnp.where(lane >= SB_DIM, q, zero)], axis=0))
    r = lax.broadcasted_iota(jnp.int32, (t, t), 0)
    c = lax.broadcasted_iota(jnp.int32, (t, t), 1)
    suffix = jnp.where(r >= c, 1.0, 0.0).astype(BF16)
    r2 = lax.broadcasted_iota(jnp.int32, (2 * t, t), 0) & (t - 1)
    c2 = lax.broadcasted_iota(jnp.int32, (2 * t, t), 1)
    strict = c2 < r2

    acc_ref[...] = jnp.zeros_like(acc_ref)
    car_ref[...] = jnp.zeros_like(car_ref)

    def block(j, diagonal):
        off = pl.multiple_of(j * t, t)
        kb = k_ref[0, pl.ds(off, t), :]
        vb = v_ref[0, pl.ds(off, t), :]
        zs = [_dot_nt(qs[p], kb[:, p * LANES:(p + 1) * LANES]) for p in range(pairs)]
        lks = []
        for z in zs:
            lk = -(jnp.maximum(z, 0.0) + jnp.log2(1.0 + jnp.exp2(-jnp.abs(z))))
            lks.append(jnp.where(strict, lk, 0.0) if diagonal else lk)
        withins = [_dot(lk.astype(BF16), suffix) for lk in lks]
        weights, cars = [], []
        for p in range(pairs):
            car = car_ref[p]
            a = jnp.exp2(jnp.minimum(zs[p] + withins[p], 0.0) + car)
            weights.append((jnp.where(strict, a, 0.0) if diagonal else a).astype(BF16))
            cars.append(car + withins[p][:, :1])
            car_ref[p] = cars[p]
        for p in range(pairs):
            acc_ref[p] += _dot(weights[p], vb[:, p * LANES:(p + 1) * LANES])
        return jnp.max(functools.reduce(jnp.maximum, cars)) > SB_LOG_FLOOR

    def live(state):
        s, alive = state
        return jnp.logical_and(s < i, alive)

    def body(state):
        s, _ = state
        return s + 1, block(i - 1 - s, False)

    lax.while_loop(live, body, (0, block(i, True)))
    for p in range(pairs):
        o_ref[0, :, p * LANES:(p + 1) * LANES] = jnp.where(
            lane < SB_DIM, acc_ref[p, :t], acc_ref[p, t:]).astype(o_ref.dtype)


def _sb_attention(q, k, v):
    b, s, _ = q.shape
    t = ATT_TILE
    pairs = SB_HEADS // 2
    return pl.pallas_call(
        _sb_kernel,
        grid=(b, s // t),
        in_specs=[pl.BlockSpec((1, t, SB_W), lambda bi, i: (bi, i, 0)),
                  pl.BlockSpec((1, s, SB_W), lambda bi, i: (bi, 0, 0)),
                  pl.BlockSpec((1, s, SB_W), lambda bi, i: (bi, 0, 0))],
        out_specs=pl.BlockSpec((1, t, SB_W), lambda bi, i: (bi, i, 0)),
        out_shape=jax.ShapeDtypeStruct((b, s, SB_W), BF16),
        scratch_shapes=[pltpu.VMEM((pairs, 2 * t, LANES), F32), pltpu.VMEM((pairs, 2 * t, 1), F32)],
        compiler_params=_params(("parallel", "arbitrary")),
        name="stick_breaking",
    )(q, k, v)


def _mla_kernel(q_ref, k_ref, v_ref, o_ref, acc_ref, *stat_refs, online):
    tq, tk = MLA_Q_TILE, MLA_K_TILE
    sub = tq // tk
    i = pl.program_id(2)
    r = lax.broadcasted_iota(jnp.int32, (tk, tk), 0)
    c = lax.broadcasted_iota(jnp.int32, (tk, tk), 1)
    causal = c <= r
    lane = lax.broadcasted_iota(jnp.int32, (tq, LANES), 1)
    reps = tk // LANES

    acc_ref[...] = jnp.zeros_like(acc_ref)
    if online:
        m_ref = stat_refs[0]
        m_ref[...] = jnp.full_like(m_ref, NEG)

    def block(j, rows, diagonal):
        off = pl.multiple_of(j * tk, tk)
        kb = k_ref[0, pl.ds(off, tk), :]
        vb = v_ref[0, pl.ds(off, tk), :]
        for h in range(2):
            sl = slice(h * MLA_PAD, (h + 1) * MLA_PAD)
            s = _dot_nt(q_ref[0, rows, sl], kb[:, sl])
            if diagonal:
                s = jnp.where(causal, s, NEG)
            if online:
                m_old = m_ref[h, rows, :]
                m_new = jnp.maximum(m_old, jnp.max(s, axis=-1, keepdims=True))
                p = jnp.exp2(s - jnp.tile(m_new, (1, reps)))
                acc_ref[h, rows, :] = (jnp.exp2(m_old - m_new) * acc_ref[h, rows, :]
                                       + _dot(p.astype(BF16), vb[:, sl]))
                m_ref[h, rows, :] = m_new
            else:
                acc_ref[h, rows, :] += _dot(jnp.exp2(s).astype(BF16), vb[:, sl])

    for a in range(sub):
        rows = slice(a * tk, (a + 1) * tk)
        for d in range(a + 1):
            block(i * sub + d, rows, d == a)

    def body(j, carry):
        block(j, slice(0, tq), False)
        return carry

    lax.fori_loop(0, i * sub, body, 0)
    a0, a1 = acc_ref[0], acc_ref[1]
    half = LANES // 2
    o0 = a0 / pltpu.roll(a0, half, 1)
    o1 = pltpu.roll(a1, half, 1) / a1
    o_ref[0] = jnp.where(lane < MLA_V, o0, o1).astype(o_ref.dtype)


def _mla_attention(q, k, v, bounded):
    b, s, _ = q.shape
    t = MLA_Q_TILE
    stat = pltpu.VMEM((2, t, LANES), F32)

    def call(online):
        return pl.pallas_call(
            functools.partial(_mla_kernel, online=online),
            grid=(b, MLA_HEADS // 2, s // t),
            in_specs=[pl.BlockSpec((1, t, 2 * MLA_PAD), lambda bi, p, i: (bi, i, p)),
                      pl.BlockSpec((1, s, 2 * MLA_PAD), lambda bi, p, i: (bi, 0, p)),
                      pl.BlockSpec((1, s, 2 * MLA_PAD), lambda bi, p, i: (bi, 0, p))],
            out_specs=pl.BlockSpec((1, t, LANES), lambda bi, p, i: (bi, i, p)),
            out_shape=jax.ShapeDtypeStruct((b, s, MLA_HEADS * MLA_V), BF16),
            scratch_shapes=[stat, stat] if online else [stat],
            compiler_params=_params(("parallel", "parallel", "arbitrary")),
            name="latent_attention_online" if online else "latent_attention",
        )

    return lax.cond(bounded, call(False), call(True), q, k, v)


def _band_kernel(*refs, heads, dil, blocks, fan, with_sinks, with_lse):
    q_ref, kp_ref, kc_ref, vp_ref, vc_ref, bias_ref, first_ref = refs[:7]
    rest = refs[7:]
    if with_sinks:
        sink_ref, rest = rest[0], rest[1:]
    o_ref = rest[0]
    width = heads * DIL_DIM
    n = pl.program_id(1)
    group = lax.broadcasted_iota(jnp.int32, (BLOCK, width), 1) // DIL_DIM
    bias = bias_ref[...]
    first_bias = jnp.where(n > 0, bias, first_ref[...])

    def rows(r, blk):
        start = r + blk * (BLOCK * dil)
        return pl.ds(start, BLOCK, stride=dil) if dil > 1 else pl.ds(start, BLOCK)

    def load(ref, r, blk):
        return ref[0, rows(r, blk), :].astype(BF16)

    def streams(r0):
        idx, vs, ss, ps, ls, ms = [], [], [], [], [], []
        for r, j in [(r0 + dr, j) for dr in range(fan) for j in range(blocks)]:
            q = load(q_ref, r, j)
            zero = jnp.zeros_like(q)
            qs = jnp.concatenate([jnp.where(group == h, q, zero) for h in range(heads)], axis=0)
            k_prev = load(kp_ref, r, 0) if j == 0 else load(kc_ref, r, j - 1)
            v_prev = load(vp_ref, r, 0) if j == 0 else load(vc_ref, r, j - 1)
            k = jnp.concatenate([k_prev, load(kc_ref, r, j)], axis=0)
            vs.append(jnp.concatenate([v_prev, load(vc_ref, r, j)], axis=0))
            ss.append(_dot_nt(k, qs) + (first_bias if j == 0 else bias))
            idx.append(rows(r, j))
        for s in ss:
            m = jnp.max(s, axis=0, keepdims=True)
            if with_sinks:
                sink = sink_ref[...]
                m = jnp.maximum(m, sink)
            p = jnp.exp2(s - m)
            l = jnp.sum(p, axis=0, keepdims=True)
            if with_sinks:
                l = l + jnp.exp2(sink - m)
            ps.append(p.astype(BF16))
            ls.append(l)
            ms.append(m)
        os_ = [_dot_tn(v, p) * (1.0 / l) for v, p, l in zip(vs, ps, ls)]
        for j, o in enumerate(os_):
            out = jnp.concatenate([o[h * DIL_DIM:(h + 1) * DIL_DIM, h * BLOCK:(h + 1) * BLOCK]
                                   for h in range(heads)], axis=0)
            o_ref[0, idx[j], :] = out.T.astype(o_ref.dtype)
            if with_lse:
                lg = (ms[j] + jnp.log2(ls[j])) * math.log(2.0)
                lse = jnp.concatenate([jnp.broadcast_to(lg[:, h * BLOCK:(h + 1) * BLOCK], (DIL_DIM, BLOCK))
                                       for h in range(heads)], axis=0)
                rest[1][0, idx[j], :] = lse.T

    if dil == fan:
        streams(0)
    else:
        def body(t, carry):
            streams(t * fan)
            return carry
        lax.fori_loop(0, dil // fan, body, 0)


def _band_attention(q, k, v, bias, sinks, dil, heads, inclusive, with_lse, out_dtype):
    b, s, width = q.shape
    blocks, fan = BAND_PLAN[dil]
    step = blocks * BLOCK * dil
    qi = np.arange(BLOCK)[None, :, None]
    kj = np.arange(2 * BLOCK)[None, None, :]
    window = ((kj >= qi) if inclusive else (kj > qi)) & (kj <= qi + BLOCK)
    table = lambda mask: jnp.where(mask, bias * math.log2(math.e), NEG).transpose(2, 0, 1).reshape(
        2 * BLOCK, heads * BLOCK)
    tables = (table(window), table(window & (kj >= BLOCK)))
    cur = pl.BlockSpec((1, step, width), lambda bi, n: (bi, n, 0))
    prev = pl.BlockSpec((1, BLOCK * dil, width), lambda bi, n: (bi, jnp.maximum(n * blocks - 1, 0), 0))
    in_specs = [cur, prev, cur, prev, cur] + [_full(t.shape) for t in tables]
    args = [q, k, k, v, v, *tables]
    if sinks is not None:
        sink_cols = jnp.repeat(sinks * math.log2(math.e), BLOCK)[None, :]
        in_specs.append(_full(sink_cols.shape))
        args.append(sink_cols)
    out_shape = [jax.ShapeDtypeStruct((b, s, width), out_dtype)]
    out_specs = [cur]
    if with_lse:
        out_shape.append(jax.ShapeDtypeStruct((b, s, width), F32))
        out_specs.append(cur)
    return pl.pallas_call(
        functools.partial(_band_kernel, heads=heads, dil=dil, blocks=blocks, fan=fan,
                          with_sinks=sinks is not None, with_lse=with_lse),
        grid=(b, s // step),
        in_specs=in_specs,
        out_specs=out_specs,
        out_shape=out_shape,
        compiler_params=_params(("parallel", "arbitrary")),
        name="band_attention",
    )(*args)


def _bias_kernel(rb_ref, bkt_ref, o_ref, *, head_patterns):
    for head, pat in enumerate(head_patterns):
        bkt = bkt_ref[pat]
        acc = jnp.zeros(bkt.shape, F32)
        for bucket in range(REL_BUCKETS):
            acc = jnp.where(bkt == bucket, rb_ref[bucket, head], acc)
        o_ref[head] = acc


def _t5_bucket(dist):
    max_exact = REL_BUCKETS // 2
    d = np.maximum(dist, 1).astype(np.float32)
    large = max_exact + (np.log(d / np.float32(max_exact)) / np.float32(math.log(REL_MAX_DIST / max_exact))
                         * np.float32(REL_BUCKETS - max_exact)).astype(np.int32)
    large = np.minimum(large, REL_BUCKETS - 1)
    return np.where(dist < max_exact, dist, large).astype(np.int32)


def _bias_tables(rel_bias):
    dist = (BLOCK + np.arange(BLOCK))[:, None] - np.arange(2 * BLOCK)[None, :]
    dist = np.maximum(dist, 0)
    buckets = jnp.asarray(np.stack([_t5_bucket(dist * dil) for _, dil in DIL_PATTERNS]))
    head_patterns = (0,) * SW_Q_HEADS + tuple(g for g in range(N_DIL) for _ in range(DIL_HEADS))
    n_heads = len(head_patterns)
    return pl.pallas_call(
        functools.partial(_bias_kernel, head_patterns=head_patterns),
        in_specs=[pl.BlockSpec(memory_space=pltpu.SMEM), _full(buckets.shape)],
        out_specs=_full((n_heads, BLOCK, 2 * BLOCK)),
        out_shape=jax.ShapeDtypeStruct((n_heads, BLOCK, 2 * BLOCK), F32),
        grid=(1,),
        name="rel_bias",
    )(rel_bias, buckets)


def _merge_kernel(*refs, with_router):
    (x_ref, glow_ref, oa_ref, ob_ref, oc_ref, d0_ref, d1_ref, d2_ref, l0_ref, l1_ref, l2_ref,
     wgb_ref, bg_ref, wbr_ref, wout_ref, g2_ref) = refs[:16]
    rest = refs[16:]
    if with_router:
        wr_hi_ref, wr_lo_ref, br_ref = rest[:3]
        rest = rest[3:]
    x1_ref, h2_ref = rest[:2]

    l0, l1, l2 = l0_ref[...], l1_ref[...], l2_ref[...]
    mx = jnp.maximum(jnp.maximum(l0, l1), l2)
    e0, e1, e2 = jnp.exp(l0 - mx), jnp.exp(l1 - mx), jnp.exp(l2 - mx)
    od = (e0 * d0_ref[...] + e1 * d1_ref[...] + e2 * d2_ref[...]) / (e0 + e1 + e2)

    glow = glow_ref[...]
    branches = (oa_ref[...], ob_ref[...], oc_ref[...], od.astype(BF16))
    chunk = MERGE_COL_CHUNK
    ys = []
    for c in range(D_MODEL // chunk):
        y = None
        row = 0
        for i, o in enumerate(branches):
            cols = slice(i * D_MODEL + c * chunk, i * D_MODEL + (c + 1) * chunk)
            pre = _dot(glow, wgb_ref[:, cols]) + bg_ref[:, cols]
            gate = 0.5 * jnp.tanh(0.5 * pre) + 0.5
            term = gate * _dot(o, wbr_ref[row:row + BR_SIZES[i], c * chunk:(c + 1) * chunk])
            y = term if y is None else y + term
            row += BR_SIZES[i]
        ys.append(y.astype(BF16))
    x1 = x_ref[...] + _dot(jnp.concatenate(ys, axis=1), wout_ref[...])
    x1_ref[...] = x1
    h2 = _rms(x1, g2_ref[...])
    h2_ref[...] = h2.astype(BF16)

    if with_router:
        wts_ref = rest[2]
        hi, lo = _split_bf16(h2)
        whi = wr_hi_ref[...]
        lg = _dot(hi, whi) + _dot(lo, whi) + _dot(hi, wr_lo_ref[...]) + br_ref[...]
        lane = lax.broadcasted_iota(jnp.int32, lg.shape, 1).astype(F32)
        big = float(LANES)
        m1 = jnp.max(lg, axis=-1, keepdims=True)
        i1 = jnp.min(jnp.where(lg == m1, lane, big), axis=-1, keepdims=True)
        lg2 = jnp.where(lane == i1, NEG, lg)
        m2 = jnp.max(lg2, axis=-1, keepdims=True)
        i2 = jnp.min(jnp.where(lg2 == m2, lane, big), axis=-1, keepdims=True)
        e = jnp.exp(m2 - m1)
        w1 = 1.0 / (1.0 + e)
        w2 = e / (1.0 + e)
        wts_ref[...] = jnp.where(lane == i1, w1, 0.0) + jnp.where(lane == i2, w2, 0.0)


def _merge(x2, glow, oa, ob, oc, dil_o, dil_l, wgb, bg, wbr, wout, g2, router):
    n = x2.shape[0]
    tm = MERGE_ROW_TILE
    row = lambda w: pl.BlockSpec((tm, w), lambda i: (i, 0))
    acts = [x2, glow, oa, ob, oc, *dil_o, *dil_l]
    weights = [wgb, bg, wbr, wout, g2] + (list(router) if router is not None else [])
    out_shape = [jax.ShapeDtypeStruct((n, D_MODEL), F32), jax.ShapeDtypeStruct((n, D_MODEL), BF16)]
    out_specs = [row(D_MODEL), row(D_MODEL)]
    if router is not None:
        out_shape.append(jax.ShapeDtypeStruct((n, LANES), F32))
        out_specs.append(row(LANES))
    return pl.pallas_call(
        functools.partial(_merge_kernel, with_router=router is not None),
        grid=(n // tm,),
        in_specs=[row(a.shape[1]) for a in acts] + [_full(w.shape) for w in weights],
        out_specs=out_specs,
        out_shape=out_shape,
        compiler_params=_params(("parallel",)),
        name="merge",
    )(*acts, *weights)


FF_CHUNK = 512


def _dense_ffn_kernel(x1_ref, h2_ref, wg_ref, wu_ref, wd_ref, o_ref):
    h2 = h2_ref[...]
    acc = x1_ref[...]
    for c in range(D_FF_DENSE // FF_CHUNK):
        cols = slice(c * FF_CHUNK, (c + 1) * FF_CHUNK)
        g = _dot(h2, wg_ref[:, cols])
        u = _dot(h2, wu_ref[:, cols])
        act = (g / (1.0 + jnp.exp(-g)) * u).astype(BF16)
        acc = acc + _dot(act, wd_ref[cols, :])
    o_ref[...] = acc


def _dense_ffn(x1, h2, wg, wu, wd):
    n = x1.shape[0]
    tm = ROW_TILE
    row = pl.BlockSpec((tm, D_MODEL), lambda i: (i, 0))
    return pl.pallas_call(
        _dense_ffn_kernel,
        grid=(n // tm,),
        in_specs=[row, row, _full(wg.shape), _full(wu.shape), _full(wd.shape)],
        out_specs=row,
        out_shape=jax.ShapeDtypeStruct((n, D_MODEL), F32),
        compiler_params=_params(("parallel",)),
        name="dense_ffn",
    )(x1, h2, wg, wu, wd)


def _moe_kernel(x1_ref, h2_ref, wts_ref, wg_ref, wu_ref, wd_ref, o_ref):
    step = pl.program_id(1)

    @pl.when(step == 0)
    def _():
        o_ref[...] = x1_ref[...]

    wts = wts_ref[...]
    lane = lax.broadcasted_iota(jnp.int32, wts.shape, 1)
    h2 = h2_ref[...]
    y = None
    for j in range(MOE_EXPERTS_PER_STEP):
        e = step * MOE_EXPERTS_PER_STEP + j
        w_e = jnp.sum(jnp.where(lane == e, wts, 0.0), axis=-1, keepdims=True)
        g = _dot(h2, wg_ref[j])
        u = _dot(h2, wu_ref[j])
        act = (g / (1.0 + jnp.exp(-g)) * u).astype(BF16)
        part = w_e * _dot(act, wd_ref[j])
        y = part if y is None else y + part
    o_ref[...] += y


def _moe_ffn(x1, h2, wts, wg, wu, wd):
    n = x1.shape[0]
    tm = MOE_ROW_TILE
    ne = MOE_EXPERTS_PER_STEP
    row = lambda w: pl.BlockSpec((tm, w), lambda i, e: (i, 0))
    return pl.pallas_call(
        _moe_kernel,
        grid=(n // tm, N_EXPERTS // ne),
        in_specs=[row(D_MODEL), row(D_MODEL), row(LANES),
                  pl.BlockSpec((ne, D_MODEL, D_FF_EXPERT), lambda i, e: (e, 0, 0)),
                  pl.BlockSpec((ne, D_MODEL, D_FF_EXPERT), lambda i, e: (e, 0, 0)),
                  pl.BlockSpec((ne, D_FF_EXPERT, D_MODEL), lambda i, e: (e, 0, 0))],
        out_specs=row(D_MODEL),
        out_shape=jax.ShapeDtypeStruct((n, D_MODEL), F32),
        compiler_params=_params(("parallel", "arbitrary")),
        name="moe_ffn",
    )(x1, h2, wts, wg, wu, wd)


def _rope_tables(seq):
    half = MLA_ROPE // 2
    inv = ROPE_THETA ** (-jnp.arange(half, dtype=F32) / half)
    ang = jnp.arange(seq, dtype=F32)[:, None] * inv[None, :]
    cos, sin = jnp.cos(ang), jnp.sin(ang)
    zeros = lambda w: jnp.zeros((seq, w), F32)
    tail = LANES - MLA_QK
    rc = jnp.concatenate([jnp.ones((seq, MLA_NOPE), F32), cos, cos, zeros(tail)], axis=1)
    rs1 = jnp.concatenate([zeros(MLA_NOPE + half), sin, zeros(tail)], axis=1)
    rs2 = jnp.concatenate([zeros(MLA_NOPE), -sin, zeros(half + tail)], axis=1)
    return rc, rs1, rs2


def _pad_heads(w, heads, dim):
    rows = w.shape[0]
    w = w.reshape(rows, heads, dim)
    return jnp.pad(w, ((0, 0), (0, 0), (0, LANES - dim))).reshape(rows, heads * LANES)


def _layer_weights(w_in, g_qa, w_qb, g_kva, w_kvb, qk_g_mla, qk_g_sw, qk_g_dil, w_gate_a):
    offs = np.concatenate([[0], np.cumsum(IN_SIZES)]).tolist()
    cols = [w_in[:, offs[j]:offs[j + 1]] for j in range(len(IN_SIZES))]
    a_q, a_k, a_v, b_cq, b_ckv, b_kpe, c_q, c_k, c_v, d_q, d_k, d_v = cols
    wa = jnp.concatenate([a_q * (SB_DIM ** -0.5), a_k, a_v], axis=1)
    kpe_pad = jnp.pad(b_kpe, ((0, 0), (MLA_NOPE, LANES - MLA_QK)))
    wb = jnp.concatenate([b_cq, b_ckv, kpe_pad], axis=1)
    rep = SW_Q_HEADS // SW_KV_HEADS
    expand = lambda w: jnp.repeat(w.reshape(D_MODEL, SW_KV_HEADS, SW_DIM), rep, axis=1).reshape(D_MODEL, SW_W)
    wc = jnp.concatenate([c_q, expand(c_k), expand(c_v)], axis=1)
    wd = jnp.concatenate([d_q, d_k, d_v], axis=1)
    wqb = _pad_heads(w_qb, MLA_HEADS, MLA_QK)
    kvb = w_kvb.reshape(MLA_KV_RANK, MLA_HEADS, MLA_NOPE + MLA_V)
    wkvk = _pad_heads(kvb[:, :, :MLA_NOPE].reshape(MLA_KV_RANK, -1), MLA_HEADS, MLA_NOPE)
    wkvv = _pad_heads(kvb[:, :, MLA_NOPE:].reshape(MLA_KV_RANK, -1), MLA_HEADS, MLA_V)
    pad_gain = lambda g: jnp.tile(jnp.pad(g, (0, LANES - MLA_QK)), MLA_HEADS)[None, :]
    gmq = pad_gain(qk_g_mla[0]) * (MLA_QK ** -0.5 * math.log2(math.e))
    gmk = pad_gain(qk_g_mla[1])
    bound = jnp.max(jnp.abs(qk_g_mla[0] * qk_g_mla[1])) * (MLA_QK ** 0.5 * math.log2(math.e))
    shift = (bound * MLA_BOUND_MARGIN).astype(BF16).astype(F32)
    bounded = shift < MLA_MAX_SHIFT
    spare = lambda lo, hi: jnp.tile(jnp.pad(jnp.ones((hi - lo,), F32), (lo, LANES - hi)), MLA_HEADS)[None, :]
    oq = jnp.where(bounded, -shift, 0.0) * spare(MLA_QK, MLA_QK + 1)
    ok = spare(MLA_QK, MLA_QK + 1)
    ov = spare(MLA_V, LANES)
    gsq = jnp.tile(qk_g_sw[0], SW_Q_HEADS)[None, :] * (SW_DIM ** -0.5 * math.log2(math.e))
    gsk = jnp.tile(qk_g_sw[1], SW_Q_HEADS)[None, :]
    gdq = jnp.tile(qk_g_dil[0], N_DIL * DIL_HEADS)[None, :] * (DIL_DIM ** -0.5 * math.log2(math.e))
    gdk = jnp.tile(qk_g_dil[1], N_DIL * DIL_HEADS)[None, :]
    bf = lambda w: w.astype(BF16)
    return (bf(wa), bf(wb), bf(wc), bf(wd), bf(w_gate_a), g_qa[None, :], bf(wqb), g_kva[None, :],
            bf(wkvk), bf(wkvv)), (gmq, gmk, gsq, gsk, gdq, gdk, oq, ok, ov), bounded


def kernel(x, norm1_g, w_in, g_qa, w_qb, g_kva, w_kvb, qk_g_mla, qk_g_sw, qk_g_dil, sinks, rel_bias, w_gate_a, w_gate_b, b_gate, w_branch, w_out, norm2_g, w_gu_dense, w_down_dense, w_router, b_router, w_gu_exp, w_down_exp):
    b, s, d = x.shape
    n = b * s
    depth = norm1_g.shape[0]
    bias = _bias_tables(rel_bias)
    sw_bias = bias[:SW_Q_HEADS]
    dil_bias = [bias[SW_Q_HEADS + g * DIL_HEADS:SW_Q_HEADS + (g + 1) * DIL_HEADS] for g in range(N_DIL)]
    rc, rs1, rs2 = _rope_tables(s)
    x2 = x.reshape(n, d)
    seq3 = lambda a: a.reshape(b, s, a.shape[-1])
    flat = lambda a: a.reshape(n, a.shape[-1])

    for i in range(depth):
        weights, gains, bounded = _layer_weights(w_in[i], g_qa[i], w_qb[i], g_kva[i], w_kvb[i],
                                                 qk_g_mla[i], qk_g_sw[i], qk_g_dil[i], w_gate_a[i])
        wa, wb, wc, wd, wg, gqa, wqb, gkva, wkvk, wkvv = weights
        (sbq, sbk, sbv, mq, mk, mv, swq, swk, swv,
         dq0, dq1, dq2, dk0, dk1, dk2, dv0, dv1, dv2, glow) = _inproj(
            x2, norm1_g[i][None, :], wa, wb, wc, wd, wg, gqa, wqb, gkva, wkvk, wkvv, rc, rs1, rs2,
            *gains, s)

        out_a = _sb_attention(seq3(sbq), seq3(sbk), seq3(sbv))
        out_b = _mla_attention(seq3(mq), seq3(mk), seq3(mv), bounded)
        out_c = _band_attention(seq3(swq), seq3(swk), seq3(swv), sw_bias, sinks[i], 1,
                                SW_Q_HEADS, False, False, BF16)[0]
        dil_o, dil_l = [], []
        for g, (dq, dk, dv) in enumerate(((dq0, dk0, dv0), (dq1, dk1, dv1), (dq2, dk2, dv2))):
            o, l = _band_attention(seq3(dq), seq3(dk), seq3(dv), dil_bias[g], None,
                                   DIL_PATTERNS[g][1], DIL_HEADS, True, True, F32)
            dil_o.append(flat(o))
            dil_l.append(flat(l))

        router = None
        if i % 2 == 1:
            wr = jnp.pad(w_router[i // 2], ((0, 0), (0, LANES - N_EXPERTS)))
            wr_hi = wr.astype(BF16)
            wr_lo = (wr - wr_hi.astype(F32)).astype(BF16)
            br = jnp.pad(b_router[i // 2], (0, LANES - N_EXPERTS), constant_values=NEG)[None, :]
            router = (wr_hi, wr_lo, br)
        merged = _merge(x2, glow, flat(out_a), flat(out_b), flat(out_c), dil_o, dil_l,
                        w_gate_b[i].astype(BF16), b_gate[i][None, :], w_branch[i].astype(BF16),
                        w_out[i].astype(BF16), norm2_g[i][None, :], router)
        if i % 2 == 0:
            x1, h2 = merged
            w_gu = w_gu_dense[i // 2].astype(BF16)
            x2 = _dense_ffn(x1, h2, w_gu[:, :D_FF_DENSE], w_gu[:, D_FF_DENSE:],
                            w_down_dense[i // 2].astype(BF16))
        else:
            x1, h2, wts = merged
            w_gu = w_gu_exp[i // 2].astype(BF16)
            x2 = _moe_ffn(x1, h2, wts, w_gu[:, :, :D_FF_EXPERT], w_gu[:, :, D_FF_EXPERT:],
                          w_down_exp[i // 2].astype(BF16))
    return x2.reshape(b, s, d)
```

```python
import functools
import math

import jax
import jax.numpy as jnp
import numpy as np
from jax import lax
from jax.experimental import pallas as pl
from jax.experimental.pallas import tpu as pltpu

F32 = jnp.float32
BF16 = jnp.bfloat16

D_MODEL = 1024
BLOCK = 128
EPS = 1e-6
SB_HEADS, SB_DIM = 4, 64
MLA_HEADS, MLA_Q_RANK, MLA_KV_RANK = 4, 256, 256
MLA_NOPE, MLA_ROPE, MLA_V = 64, 32, 64
MLA_QK = MLA_NOPE + MLA_ROPE
ROPE_THETA = 10000.0
SW_Q_HEADS, SW_KV_HEADS, SW_DIM = 8, 2, 32
DIL_PATTERNS = ((128, 1), (512, 4), (2048, 16))
DIL_HEADS, DIL_DIM = 4, 32
N_DIL = len(DIL_PATTERNS)
REL_BUCKETS, REL_MAX_DIST = 32, 2048
GATE_RANK, N_BRANCH = 128, 4
D_FF_DENSE = 2048
N_EXPERTS, TOP_K, D_FF_EXPERT = 8, 2, 768

SB_W = SB_HEADS * SB_DIM
SW_W = SW_Q_HEADS * SW_DIM
SW_KV_W = SW_KV_HEADS * SW_DIM
DIL_GW = DIL_HEADS * DIL_DIM
DIL_W = N_DIL * DIL_GW
IN_SIZES = (SB_W, SB_W, SB_W, MLA_Q_RANK, MLA_KV_RANK, MLA_ROPE, SW_W, SW_KV_W, SW_KV_W, DIL_W, DIL_W, DIL_W)
BR_SIZES = (SB_W, MLA_HEADS * MLA_V, SW_W, DIL_GW)

LANES = 128
MLA_PAD = LANES
NEG = -1e30
VMEM_LIMIT = 56 * 1024 * 1024

ROW_TILE = 1024
MERGE_COL_CHUNK = 256
MERGE_ROW_TILE = 1024
MOE_ROW_TILE = 1024
MOE_EXPERTS_PER_STEP = 2
ATT_TILE = 256
MLA_Q_TILE = 2048
MLA_K_TILE = 512
SB_LOG_FLOOR = -152.0
MLA_BOUND_MARGIN = 1.03
MLA_MAX_SHIFT = 60.0
BAND_PLAN = {1: (16, 1), 4: (8, 2), 16: (2, 8)}


def _dot(a, b):
    return jnp.dot(a, b, preferred_element_type=F32)


def _dot_nt(a, b):
    return lax.dot_general(a, b, (((1,), (1,)), ((), ())), preferred_element_type=F32)


def _dot_tn(a, b):
    return lax.dot_general(a, b, (((0,), (0,)), ((), ())), preferred_element_type=F32)


def _params(sem):
    return pltpu.CompilerParams(dimension_semantics=sem, vmem_limit_bytes=VMEM_LIMIT)


def _full(shape):
    nd = len(shape)
    return pl.BlockSpec(shape, lambda *_: (0,) * nd)


def _rms(v, g):
    return v * lax.rsqrt(jnp.mean(v * v, axis=-1, keepdims=True) + EPS) * g


def _split_bf16(v):
    hi = v.astype(BF16)
    lo = (v - hi.astype(F32)).astype(BF16)
    return hi, lo


def _group_matrix(group):
    r = (lax.broadcasted_iota(jnp.int32, (2 * LANES, LANES), 0) & (LANES - 1)) // group
    c = lax.broadcasted_iota(jnp.int32, (2 * LANES, LANES), 1) // group
    return jnp.where(r == c, 1.0, 0.0).astype(BF16)


def _group_norm_slab(xs, gmat, inv_n):
    hi, lo = _split_bf16(xs * xs)
    ss = _dot(jnp.concatenate([hi, lo], axis=1), gmat)
    return xs * lax.rsqrt(ss * inv_n + EPS)


def _inproj_kernel(x_ref, g1_ref, wa_ref, wb_ref, wc_ref, wd_ref, wg_ref,
                   gqa_ref, wqb_ref, gkva_ref, wkvk_ref, wkvv_ref,
                   rc_ref, rs1_ref, rs2_ref,
                   gmq_ref, gmk_ref, gsq_ref, gsk_ref, gdq_ref, gdk_ref, oq_ref, ok_ref, ov_ref,
                   sbq, sbk, sbv, mq, mk, mv, swq, swk, swv,
                   dq0, dq1, dq2, dk0, dk1, dk2, dv0, dv1, dv2, glow):
    x = x_ref[...]
    h = _rms(x, g1_ref[...]).astype(BF16)

    pa = _dot(h, wa_ref[...])
    sbq[...] = (pa[:, :SB_W] * math.log2(math.e)).astype(BF16)
    sbk[...] = pa[:, SB_W:2 * SB_W].astype(BF16)
    sbv[...] = pa[:, 2 * SB_W:].astype(BF16)

    pb = _dot(h, wb_ref[...])
    cq = pb[:, :MLA_Q_RANK]
    ckv = pb[:, MLA_Q_RANK:MLA_Q_RANK + MLA_KV_RANK]
    kpe = pb[:, MLA_Q_RANK + MLA_KV_RANK:]
    half = MLA_ROPE // 2

    def rope(v, reps):
        tile = lambda t: jnp.concatenate([t] * reps, axis=1) if reps > 1 else t
        width = reps * LANES
        return (v * tile(rc_ref[...]) + pltpu.roll(v, half, 1) * tile(rs1_ref[...])
                + pltpu.roll(v, width - half, 1) * tile(rs2_ref[...]))

    ones = _group_matrix(LANES)
    q = rope(_dot(_rms(cq, gqa_ref[...]).astype(BF16), wqb_ref[...]), MLA_HEADS)
    kvn = _rms(ckv, gkva_ref[...]).astype(BF16)
    k = _dot(kvn, wkvk_ref[...]) + jnp.concatenate([rope(kpe, 1)] * MLA_HEADS, axis=1)
    mv[...] = (_dot(kvn, wkvv_ref[...]) + ov_ref[...]).astype(BF16)
    gmq = gmq_ref[...]
    gmk = gmk_ref[...]
    oq = oq_ref[...]
    ok = ok_ref[...]
    for s in range(MLA_HEADS):
        sl = slice(s * LANES, (s + 1) * LANES)
        mq[:, sl] = (_group_norm_slab(q[:, sl], ones, 1.0 / MLA_QK) * gmq[:, sl] + oq[:, sl]).astype(BF16)
        mk[:, sl] = (_group_norm_slab(k[:, sl], ones, 1.0 / MLA_QK) * gmk[:, sl] + ok[:, sl]).astype(BF16)

    g32 = _group_matrix(SW_DIM)
    pc = _dot(h, wc_ref[...])
    gsq = gsq_ref[...]
    gsk = gsk_ref[...]
    for s in range(SW_W // LANES):
        sl = slice(s * LANES, (s + 1) * LANES)
        ks = slice(SW_W + s * LANES, SW_W + (s + 1) * LANES)
        swq[:, sl] = (_group_norm_slab(pc[:, sl], g32, 1.0 / SW_DIM) * gsq[:, sl]).astype(BF16)
        swk[:, sl] = (_group_norm_slab(pc[:, ks], g32, 1.0 / SW_DIM) * gsk[:, sl]).astype(BF16)
    swv[...] = pc[:, 2 * SW_W:].astype(BF16)

    pd = _dot(h, wd_ref[...])
    gdq = gdq_ref[...]
    gdk = gdk_ref[...]
    for g, (q_out, k_out, v_out) in enumerate(((dq0, dk0, dv0), (dq1, dk1, dv1), (dq2, dk2, dv2))):
        sl = slice(g * LANES, (g + 1) * LANES)
        ks = slice(DIL_W + g * LANES, DIL_W + (g + 1) * LANES)
        vs = slice(2 * DIL_W + g * LANES, 2 * DIL_W + (g + 1) * LANES)
        q_out[...] = _group_norm_slab(pd[:, sl], g32, 1.0 / DIL_DIM) * gdq[:, sl]
        k_out[...] = _group_norm_slab(pd[:, ks], g32, 1.0 / DIL_DIM) * gdk[:, sl]
        v_out[...] = pd[:, vs]

    glow[...] = _dot(h, wg_ref[...]).astype(BF16)


def _inproj(x2, g1, wa, wb, wc, wd, wg, gqa, wqb, gkva, wkvk, wkvv, rc, rs1, rs2,
            gmq, gmk, gsq, gsk, gdq, gdk, oq, ok, ov, seq):
    n = x2.shape[0]
    tm = ROW_TILE
    n_seq_tiles = seq // tm
    row = lambda w: pl.BlockSpec((tm, w), lambda i: (i, 0))
    pos = lambda w: pl.BlockSpec((tm, w), lambda i: (i % n_seq_tiles, 0))
    weights = (g1, wa, wb, wc, wd, wg, gqa, wqb, gkva, wkvk, wkvv)
    gains = (gmq, gmk, gsq, gsk, gdq, gdk, oq, ok, ov)
    outs = [(SB_W, BF16)] * 3 + [(MLA_HEADS * MLA_PAD, BF16)] * 3 + [(SW_W, BF16)] * 3 \
        + [(DIL_GW, F32)] * 9 + [(GATE_RANK, BF16)]
    return pl.pallas_call(
        _inproj_kernel,
        grid=(n // tm,),
        in_specs=[row(D_MODEL)] + [_full(w.shape) for w in weights] + [pos(LANES)] * 3
        + [_full(g.shape) for g in gains],
        out_specs=[row(w) for w, _ in outs],
        out_shape=[jax.ShapeDtypeStruct((n, w), dt) for w, dt in outs],
        compiler_params=_params(("parallel",)),
        name="inproj",
    )(x2, *weights, rc, rs1, rs2, *gains)


def _sb_kernel(q_ref, k_ref, v_ref, o_ref, acc_ref, car_ref):
    t = ATT_TILE
    pairs = SB_HEADS // 2
    i = pl.program_id(1)
    lane = lax.broadcasted_iota(jnp.int32, (t, LANES), 1)
    qs = []
    for p in range(pairs):
        q = q_ref[0, :, p * LANES:(p + 1) * LANES]
        zero = jnp.zeros_like(q)
        qs.append(jnp.concatenate([jnp.where(lane < SB_DIM, q, zero), jnp.where(lane >= SB_DIM, q, zero)], axis=0))
    r = lax.broadcasted_iota(jnp.int32, (t, t), 0)
    c = lax.broadcasted_iota(jnp.int32, (t, t), 1)
    suffix = jnp.where(r >= c, 1.0, 0.0).astype(BF16)
    r2 = lax.broadcasted_iota(jnp.int32, (2 * t, t), 0) & (t - 1)
    c2 = lax.broadcasted_iota(jnp.int32, (2 * t, t), 1)
    strict = c2 < r2

    acc_ref[...] = jnp.zeros_like(acc_ref)
    car_ref[...] = jnp.zeros_like(car_ref)

    def blocks(j, count, diagonal):
        kbs, vbs = [], []
        for d in range(count):
            off = pl.multiple_of((j - d) * t, t)
            kbs.append(k_ref[0, pl.ds(off, t), :])
            vbs.append(v_ref[0, pl.ds(off, t), :])
        chains = [(p, d) for p in range(pairs) for d in range(count)]
        zs = [_dot_nt(qs[p], kbs[d][:, p * LANES:(p + 1) * LANES]) for p, d in chains]
        lks = []
        for z, (_, d) in zip(zs, chains):
            lk = -(jnp.maximum(z, 0.0) + jnp.log2(1.0 + jnp.exp2(-jnp.abs(z))))
            lks.append(jnp.where(strict, lk, 0.0) if diagonal and d == 0 else lk)
        withins = [_dot(lk.astype(BF16), suffix) for lk in lks]
        cars = [car_ref[p] for p in range(pairs)]
        weights = []
        for c, (p, d) in enumerate(chains):
            a = jnp.exp2(jnp.minimum(zs[c] + withins[c], 0.0) + cars[p])
            weights.append((jnp.where(strict, a, 0.0) if diagonal and d == 0 else a).astype(BF16))
            cars[p] = cars[p] + withins[c][:, :1]
        for p in range(pairs):
            car_ref[p] = cars[p]
        for c, (p, d) in enumerate(chains):
            acc_ref[p] += _dot(weights[c], vbs[d][:, p * LANES:(p + 1) * LANES])
        return jnp.max(functools.reduce(jnp.maximum, cars)) > SB_LOG_FLOOR

    first = jnp.minimum(i + 1, 2)
    alive = lax.cond(i >= 1, lambda: blocks(i, 2, True), lambda: blocks(i, 1, True))

    def live(state):
        s, alive = state
        return jnp.logical_and(s <= i, alive)

    def body(state):
        s, _ = state
        return s + 1, blocks(i - s, 1, False)

    lax.while_loop(live, body, (first, alive))
    for p in range(pairs):
        o_ref[0, :, p * LANES:(p + 1) * LANES] = jnp.where(
            lane < SB_DIM, acc_ref[p, :t], acc_ref[p, t:]).astype(o_ref.dtype)


def _sb_attention(q, k, v):
    b, s, _ = q.shape
    t = ATT_TILE
    pairs = SB_HEADS // 2
    return pl.pallas_call(
        _sb_kernel,
        grid=(b, s // t),
        in_specs=[pl.BlockSpec((1, t, SB_W), lambda bi, i: (bi, i, 0)),
                  pl.BlockSpec((1, s, SB_W), lambda bi, i: (bi, 0, 0)),
                  pl.BlockSpec((1, s, SB_W), lambda bi, i: (bi, 0, 0))],
        out_specs=pl.BlockSpec((1, t, SB_W), lambda bi, i: (bi, i, 0)),
        out_shape=jax.ShapeDtypeStruct((b, s, SB_W), BF16),
        scratch_shapes=[pltpu.VMEM((pairs, 2 * t, LANES), F32), pltpu.VMEM((pairs, 2 * t, 1), F32)],
        compiler_params=_params(("parallel", "arbitrary")),
        name="stick_breaking",
    )(q, k, v)


def _mla_kernel(q_ref, k_ref, v_ref, o_ref, acc_ref, *stat_refs, online):
    tq, tk = MLA_Q_TILE, MLA_K_TILE
    sub = tq // tk
    i = pl.program_id(2)
    r = lax.broadcasted_iota(jnp.int32, (tk, tk), 0)
    c = lax.broadcasted_iota(jnp.int32, (tk, tk), 1)
    causal = c <= r
    lane = lax.broadcasted_iota(jnp.int32, (tq, LANES), 1)
    reps = tk // LANES

    acc_ref[...] = jnp.zeros_like(acc_ref)
    if online:
        m_ref = stat_refs[0]
        m_ref[...] = jnp.full_like(m_ref, NEG)

    def block(j, rows, diagonal):
        off = pl.multiple_of(j * tk, tk)
        kb = k_ref[0, pl.ds(off, tk), :]
        vb = v_ref[0, pl.ds(off, tk), :]
        for h in range(2):
            sl = slice(h * MLA_PAD, (h + 1) * MLA_PAD)
            s = _dot_nt(q_ref[0, rows, sl], kb[:, sl])
            if diagonal:
                s = jnp.where(causal, s, NEG)
            if online:
                m_old = m_ref[h, rows, :]
                m_new = jnp.maximum(m_old, jnp.max(s, axis=-1, keepdims=True))
                p = jnp.exp2(s - jnp.tile(m_new, (1, reps)))
                acc_ref[h, rows, :] = (jnp.exp2(m_old - m_new) * acc_ref[h, rows, :]
                                       + _dot(p.astype(BF16), vb[:, sl]))
                m_ref[h, rows, :] = m_new
            else:
                acc_ref[h, rows, :] += _dot(jnp.exp2(s).astype(BF16), vb[:, sl])

    for a in range(sub):
        rows = slice(a * tk, (a + 1) * tk)
        for d in range(a + 1):
            block(i * sub + d, rows, d == a)

    def body(j, carry):
        block(j, slice(0, tq), False)
        return carry

    lax.fori_loop(0, i * sub, body, 0)
    a0, a1 = acc_ref[0], acc_ref[1]
    half = LANES // 2
    o0 = a0 / pltpu.roll(a0, half, 1)
    o1 = pltpu.roll(a1, half, 1) / a1
    o_ref[0] = jnp.where(lane < MLA_V, o0, o1).astype(o_ref.dtype)


def _mla_attention(q, k, v, bounded):
    b, s, _ = q.shape
    t = MLA_Q_TILE
    stat = pltpu.VMEM((2, t, LANES), F32)

    def call(online):
        return pl.pallas_call(
            functools.partial(_mla_kernel, online=online),
            grid=(b, MLA_HEADS // 2, s // t),
            in_specs=[pl.BlockSpec((1, t, 2 * MLA_PAD), lambda bi, p, i: (bi, i, p)),
                      pl.BlockSpec((1, s, 2 * MLA_PAD), lambda bi, p, i: (bi, 0, p)),
                      pl.BlockSpec((1, s, 2 * MLA_PAD), lambda bi, p, i: (bi, 0, p))],
            out_specs=pl.BlockSpec((1, t, LANES), lambda bi, p, i: (bi, i, p)),
            out_shape=jax.ShapeDtypeStruct((b, s, MLA_HEADS * MLA_V), BF16),
            scratch_shapes=[stat, stat] if online else [stat],
            compiler_params=_params(("parallel", "parallel", "arbitrary")),
            name="latent_attention_online" if online else "latent_attention",
        )

    return lax.cond(bounded, call(False), call(True), q, k, v)


def _band_kernel(*refs, heads, dil, blocks, fan, with_sinks, with_lse):
    q_ref, kp_ref, kc_ref, vp_ref, vc_ref, bias_ref, first_ref = refs[:7]
    rest = refs[7:]
    if with_sinks:
        sink_ref, rest = rest[0], rest[1:]
    o_ref = rest[0]
    width = heads * DIL_DIM
    n = pl.program_id(1)
    group = lax.broadcasted_iota(jnp.int32, (BLOCK, LANES), 1) // DIL_DIM
    bias = bias_ref[...]
    first_bias = jnp.where(n > 0, bias, first_ref[...])

    def rows(r, blk):
        start = r + blk * (BLOCK * dil)
        return pl.ds(start, BLOCK, stride=dil) if dil > 1 else pl.ds(start, BLOCK)

    def load(ref, r, blk):
        return ref[0, rows(r, blk), :].astype(BF16)

    def streams(r0):
        slabs = width // LANES
        hs = heads // slabs
        cols = hs * BLOCK
        idx, vs, ss, ps, ls, ms = [], [], [], [], [], []
        for r, j in [(r0 + dr, j) for dr in range(fan) for j in range(blocks)]:
            q_all = load(q_ref, r, j)
            k_all = jnp.concatenate([load(kp_ref, r, 0) if j == 0 else load(kc_ref, r, j - 1),
                                     load(kc_ref, r, j)], axis=0)
            v_all = jnp.concatenate([load(vp_ref, r, 0) if j == 0 else load(vc_ref, r, j - 1),
                                     load(vc_ref, r, j)], axis=0)
            for g in range(slabs):
                lanes = slice(g * LANES, (g + 1) * LANES)
                q = q_all[:, lanes]
                zero = jnp.zeros_like(q)
                qs = jnp.concatenate([jnp.where(group == h, q, zero) for h in range(hs)], axis=0)
                vs.append(v_all[:, lanes])
                table = first_bias if j == 0 else bias
                ss.append(_dot_nt(k_all[:, lanes], qs) + table[:, g * cols:(g + 1) * cols])
                idx.append((rows(r, j), lanes, g))
        for s, (_, _, g) in zip(ss, idx):
            m = jnp.max(s, axis=0, keepdims=True)
            if with_sinks:
                sink = sink_ref[:, g * cols:(g + 1) * cols]
                m = jnp.maximum(m, sink)
            p = jnp.exp2(s - m)
            l = jnp.sum(p, axis=0, keepdims=True)
            if with_sinks:
                l = l + jnp.exp2(sink - m)
            ps.append(p.astype(BF16))
            ls.append(l)
            ms.append(m)
        os_ = [_dot_tn(v, p) * (1.0 / l) for v, p, l in zip(vs, ps, ls)]
        for c, o in enumerate(os_):
            rws, lanes, _ = idx[c]
            out = jnp.concatenate([o[h * DIL_DIM:(h + 1) * DIL_DIM, h * BLOCK:(h + 1) * BLOCK]
                                   for h in range(hs)], axis=0)
            o_ref[0, rws, lanes] = out.T.astype(o_ref.dtype)
            if with_lse:
                lg = (ms[c] + jnp.log2(ls[c])) * math.log(2.0)
                lse = jnp.concatenate([jnp.broadcast_to(lg[:, h * BLOCK:(h + 1) * BLOCK], (DIL_DIM, BLOCK))
                                       for h in range(hs)], axis=0)
                rest[1][0, rws, lanes] = lse.T

    if dil == fan:
        streams(0)
    else:
        def body(t, carry):
            streams(t * fan)
            return carry
        lax.fori_loop(0, dil // fan, body, 0)


def _band_attention(q, k, v, bias, sinks, dil, heads, inclusive, with_lse, out_dtype):
    b, s, width = q.shape
    blocks, fan = BAND_PLAN[dil]
    blocks = blocks // (width // LANES)
    step = blocks * BLOCK * dil
    qi = np.arange(BLOCK)[None, :, None]
    kj = np.arange(2 * BLOCK)[None, None, :]
    window = ((kj >= qi) if inclusive else (kj > qi)) & (kj <= qi + BLOCK)
    table = lambda mask: jnp.where(mask, bias * math.log2(math.e), NEG).transpose(2, 0, 1).reshape(
        2 * BLOCK, heads * BLOCK)
    tables = (table(window), table(window & (kj >= BLOCK)))
    cur = pl.BlockSpec((1, step, width), lambda bi, n: (bi, n, 0))
    prev = pl.BlockSpec((1, BLOCK * dil, width), lambda bi, n: (bi, jnp.maximum(n * blocks - 1, 0), 0))
    in_specs = [cur, prev, cur, prev, cur] + [_full(t.shape) for t in tables]
    args = [q, k, k, v, v, *tables]
    if sinks is not None:
        sink_cols = jnp.repeat(sinks * math.log2(math.e), BLOCK)[None, :]
        in_specs.append(_full(sink_cols.shape))
        args.append(sink_cols)
    out_shape = [jax.ShapeDtypeStruct((b, s, width), out_dtype)]
    out_specs = [cur]
    if with_lse:
        out_shape.append(jax.ShapeDtypeStruct((b, s, width), F32))
        out_specs.append(cur)
    return pl.pallas_call(
        functools.partial(_band_kernel, heads=heads, dil=dil, blocks=blocks, fan=fan,
                          with_sinks=sinks is not None, with_lse=with_lse),
        grid=(b, s // step),
        in_specs=in_specs,
        out_specs=out_specs,
        out_shape=out_shape,
        compiler_params=_params(("parallel", "arbitrary")),
        name="band_attention",
    )(*args)


def _bias_kernel(rb_ref, bkt_ref, o_ref, *, head_patterns):
    for head, pat in enumerate(head_patterns):
        bkt = bkt_ref[pat]
        acc = jnp.zeros(bkt.shape, F32)
        for bucket in range(REL_BUCKETS):
            acc = jnp.where(bkt == bucket, rb_ref[bucket, head], acc)
        o_ref[head] = acc


def _t5_bucket(dist):
    max_exact = REL_BUCKETS // 2
    d = np.maximum(dist, 1).astype(np.float32)
    large = max_exact + (np.log(d / np.float32(max_exact)) / np.float32(math.log(REL_MAX_DIST / max_exact))
                         * np.float32(REL_BUCKETS - max_exact)).astype(np.int32)
    large = np.minimum(large, REL_BUCKETS - 1)
    return np.where(dist < max_exact, dist, large).astype(np.int32)


def _bias_tables(rel_bias):
    dist = (BLOCK + np.arange(BLOCK))[:, None] - np.arange(2 * BLOCK)[None, :]
    dist = np.maximum(dist, 0)
    buckets = jnp.asarray(np.stack([_t5_bucket(dist * dil) for _, dil in DIL_PATTERNS]))
    head_patterns = (0,) * SW_Q_HEADS + tuple(g for g in range(N_DIL) for _ in range(DIL_HEADS))
    n_heads = len(head_patterns)
    return pl.pallas_call(
        functools.partial(_bias_kernel, head_patterns=head_patterns),
        in_specs=[pl.BlockSpec(memory_space=pltpu.SMEM), _full(buckets.shape)],
        out_specs=_full((n_heads, BLOCK, 2 * BLOCK)),
        out_shape=jax.ShapeDtypeStruct((n_heads, BLOCK, 2 * BLOCK), F32),
        grid=(1,),
        name="rel_bias",
    )(rel_bias, buckets)


def _merge_kernel(*refs, with_router):
    (x_ref, glow_ref, oa_ref, ob_ref, oc_ref, d0_ref, d1_ref, d2_ref, l0_ref, l1_ref, l2_ref,
     wgb_ref, bg_ref, wbr_ref, wout_ref, g2_ref) = refs[:16]
    rest = refs[16:]
    if with_router:
        wr_hi_ref, wr_lo_ref, br_ref = rest[:3]
        rest = rest[3:]
    x1_ref, h2_ref = rest[:2]

    l0, l1, l2 = l0_ref[...], l1_ref[...], l2_ref[...]
    mx = jnp.maximum(jnp.maximum(l0, l1), l2)
    e0, e1, e2 = jnp.exp(l0 - mx), jnp.exp(l1 - mx), jnp.exp(l2 - mx)
    od = (e0 * d0_ref[...] + e1 * d1_ref[...] + e2 * d2_ref[...]) / (e0 + e1 + e2)

    glow = glow_ref[...]
    branches = (oa_ref[...], ob_ref[...], oc_ref[...], od.astype(BF16))
    chunk = MERGE_COL_CHUNK
    ys = []
    for c in range(D_MODEL // chunk):
        y = None
        row = 0
        for i, o in enumerate(branches):
            cols = slice(i * D_MODEL + c * chunk, i * D_MODEL + (c + 1) * chunk)
            pre = _dot(glow, wgb_ref[:, cols]) + bg_ref[:, cols]
            gate = 0.5 * jnp.tanh(0.5 * pre) + 0.5
            term = gate * _dot(o, wbr_ref[row:row + BR_SIZES[i], c * chunk:(c + 1) * chunk])
            y = term if y is None else y + term
            row += BR_SIZES[i]
        ys.append(y.astype(BF16))
    x1 = x_ref[...] + _dot(jnp.concatenate(ys, axis=1), wout_ref[...])
    x1_ref[...] = x1
    h2 = _rms(x1, g2_ref[...])
    h2_ref[...] = h2.astype(BF16)

    if with_router:
        wts_ref = rest[2]
        hi, lo = _split_bf16(h2)
        whi = wr_hi_ref[...]
        lg = _dot(hi, whi) + _dot(lo, whi) + _dot(hi, wr_lo_ref[...]) + br_ref[...]
        lane = lax.broadcasted_iota(jnp.int32, lg.shape, 1).astype(F32)
        big = float(LANES)
        m1 = jnp.max(lg, axis=-1, keepdims=True)
        i1 = jnp.min(jnp.where(lg == m1, lane, big), axis=-1, keepdims=True)
        lg2 = jnp.where(lane == i1, NEG, lg)
        m2 = jnp.max(lg2, axis=-1, keepdims=True)
        i2 = jnp.min(jnp.where(lg2 == m2, lane, big), axis=-1, keepdims=True)
        e = jnp.exp(m2 - m1)
        w1 = 1.0 / (1.0 + e)
        w2 = e / (1.0 + e)
        wts_ref[...] = jnp.where(lane == i1, w1, 0.0) + jnp.where(lane == i2, w2, 0.0)


def _merge(x2, glow, oa, ob, oc, dil_o, dil_l, wgb, bg, wbr, wout, g2, router):
    n = x2.shape[0]
    tm = MERGE_ROW_TILE
    row = lambda w: pl.BlockSpec((tm, w), lambda i: (i, 0))
    acts = [x2, glow, oa, ob, oc, *dil_o, *dil_l]
    weights = [wgb, bg, wbr, wout, g2] + (list(router) if router is not None else [])
    out_shape = [jax.ShapeDtypeStruct((n, D_MODEL), F32), jax.ShapeDtypeStruct((n, D_MODEL), BF16)]
    out_specs = [row(D_MODEL), row(D_MODEL)]
    if router is not None:
        out_shape.append(jax.ShapeDtypeStruct((n, LANES), F32))
        out_specs.append(row(LANES))
    return pl.pallas_call(
        functools.partial(_merge_kernel, with_router=router is not None),
        grid=(n // tm,),
        in_specs=[row(a.shape[1]) for a in acts] + [_full(w.shape) for w in weights],
        out_specs=out_specs,
        out_shape=out_shape,
        compiler_params=_params(("parallel",)),
        name="merge",
    )(*acts, *weights)


FF_CHUNK = 512


def _dense_ffn_kernel(x1_ref, h2_ref, wg_ref, wu_ref, wd_ref, o_ref):
    h2 = h2_ref[...]
    acc = x1_ref[...]
    for c in range(D_FF_DENSE // FF_CHUNK):
        cols = slice(c * FF_CHUNK, (c + 1) * FF_CHUNK)
        g = _dot(h2, wg_ref[:, cols])
        u = _dot(h2, wu_ref[:, cols])
        act = (g / (1.0 + jnp.exp(-g)) * u).astype(BF16)
        acc = acc + _dot(act, wd_ref[cols, :])
    o_ref[...] = acc


def _dense_ffn(x1, h2, wg, wu, wd):
    n = x1.shape[0]
    tm = ROW_TILE
    row = pl.BlockSpec((tm, D_MODEL), lambda i: (i, 0))
    return pl.pallas_call(
        _dense_ffn_kernel,
        grid=(n // tm,),
        in_specs=[row, row, _full(wg.shape), _full(wu.shape), _full(wd.shape)],
        out_specs=row,
        out_shape=jax.ShapeDtypeStruct((n, D_MODEL), F32),
        compiler_params=_params(("parallel",)),
        name="dense_ffn",
    )(x1, h2, wg, wu, wd)


def _moe_kernel(x1_ref, h2_ref, wts_ref, wg_ref, wu_ref, wd_ref, o_ref):
    step = pl.program_id(1)

    @pl.when(step == 0)
    def _():
        o_ref[...] = x1_ref[...]

    wts = wts_ref[...]
    lane = lax.broadcasted_iota(jnp.int32, wts.shape, 1)
    h2 = h2_ref[...]
    y = None
    for j in range(MOE_EXPERTS_PER_STEP):
        e = step * MOE_EXPERTS_PER_STEP + j
        w_e = jnp.sum(jnp.where(lane == e, wts, 0.0), axis=-1, keepdims=True)
        g = _dot(h2, wg_ref[j])
        u = _dot(h2, wu_ref[j])
        act = (g / (1.0 + jnp.exp(-g)) * u).astype(BF16)
        part = w_e * _dot(act, wd_ref[j])
        y = part if y is None else y + part
    o_ref[...] += y


def _moe_ffn(x1, h2, wts, wg, wu, wd):
    n = x1.shape[0]
    tm = MOE_ROW_TILE
    ne = MOE_EXPERTS_PER_STEP
    row = lambda w: pl.BlockSpec((tm, w), lambda i, e: (i, 0))
    return pl.pallas_call(
        _moe_kernel,
        grid=(n // tm, N_EXPERTS // ne),
        in_specs=[row(D_MODEL), row(D_MODEL), row(LANES),
                  pl.BlockSpec((ne, D_MODEL, D_FF_EXPERT), lambda i, e: (e, 0, 0)),
                  pl.BlockSpec((ne, D_MODEL, D_FF_EXPERT), lambda i, e: (e, 0, 0)),
                  pl.BlockSpec((ne, D_FF_EXPERT, D_MODEL), lambda i, e: (e, 0, 0))],
        out_specs=row(D_MODEL),
        out_shape=jax.ShapeDtypeStruct((n, D_MODEL), F32),
        compiler_params=_params(("parallel", "arbitrary")),
        name="moe_ffn",
    )(x1, h2, wts, wg, wu, wd)


def _rope_tables(seq):
    half = MLA_ROPE // 2
    inv = ROPE_THETA ** (-jnp.arange(half, dtype=F32) / half)
    ang = jnp.arange(seq, dtype=F32)[:, None] * inv[None, :]
    cos, sin = jnp.cos(ang), jnp.sin(ang)
    zeros = lambda w: jnp.zeros((seq, w), F32)
    tail = LANES - MLA_QK
    rc = jnp.concatenate([jnp.ones((seq, MLA_NOPE), F32), cos, cos, zeros(tail)], axis=1)
    rs1 = jnp.concatenate([zeros(MLA_NOPE + half), sin, zeros(tail)], axis=1)
    rs2 = jnp.concatenate([zeros(MLA_NOPE), -sin, zeros(half + tail)], axis=1)
    return rc, rs1, rs2


def _pad_heads(w, heads, dim):
    rows = w.shape[0]
    w = w.reshape(rows, heads, dim)
    return jnp.pad(w, ((0, 0), (0, 0), (0, LANES - dim))).reshape(rows, heads * LANES)


def _layer_weights(w_in, g_qa, w_qb, g_kva, w_kvb, qk_g_mla, qk_g_sw, qk_g_dil, w_gate_a):
    offs = np.concatenate([[0], np.cumsum(IN_SIZES)]).tolist()
    cols = [w_in[:, offs[j]:offs[j + 1]] for j in range(len(IN_SIZES))]
    a_q, a_k, a_v, b_cq, b_ckv, b_kpe, c_q, c_k, c_v, d_q, d_k, d_v = cols
    wa = jnp.concatenate([a_q * (SB_DIM ** -0.5), a_k, a_v], axis=1)
    kpe_pad = jnp.pad(b_kpe, ((0, 0), (MLA_NOPE, LANES - MLA_QK)))
    wb = jnp.concatenate([b_cq, b_ckv, kpe_pad], axis=1)
    rep = SW_Q_HEADS // SW_KV_HEADS
    expand = lambda w: jnp.repeat(w.reshape(D_MODEL, SW_KV_HEADS, SW_DIM), rep, axis=1).reshape(D_MODEL, SW_W)
    wc = jnp.concatenate([c_q, expand(c_k), expand(c_v)], axis=1)
    wd = jnp.concatenate([d_q, d_k, d_v], axis=1)
    wqb = _pad_heads(w_qb, MLA_HEADS, MLA_QK)
    kvb = w_kvb.reshape(MLA_KV_RANK, MLA_HEADS, MLA_NOPE + MLA_V)
    wkvk = _pad_heads(kvb[:, :, :MLA_NOPE].reshape(MLA_KV_RANK, -1), MLA_HEADS, MLA_NOPE)
    wkvv = _pad_heads(kvb[:, :, MLA_NOPE:].reshape(MLA_KV_RANK, -1), MLA_HEADS, MLA_V)
    pad_gain = lambda g: jnp.tile(jnp.pad(g, (0, LANES - MLA_QK)), MLA_HEADS)[None, :]
    gmq = pad_gain(qk_g_mla[0]) * (MLA_QK ** -0.5 * math.log2(math.e))
    gmk = pad_gain(qk_g_mla[1])
    bound = jnp.max(jnp.abs(qk_g_mla[0] * qk_g_mla[1])) * (MLA_QK ** 0.5 * math.log2(math.e))
    shift = (bound * MLA_BOUND_MARGIN).astype(BF16).astype(F32)
    bounded = shift < MLA_MAX_SHIFT
    spare = lambda lo, hi: jnp.tile(jnp.pad(jnp.ones((hi - lo,), F32), (lo, LANES - hi)), MLA_HEADS)[None, :]
    oq = jnp.where(bounded, -shift, 0.0) * spare(MLA_QK, MLA_QK + 1)
    ok = spare(MLA_QK, MLA_QK + 1)
    ov = spare(MLA_V, LANES)
    gsq = jnp.tile(qk_g_sw[0], SW_Q_HEADS)[None, :] * (SW_DIM ** -0.5 * math.log2(math.e))
    gsk = jnp.tile(qk_g_sw[1], SW_Q_HEADS)[None, :]
    gdq = jnp.tile(qk_g_dil[0], N_DIL * DIL_HEADS)[None, :] * (DIL_DIM ** -0.5 * math.log2(math.e))
    gdk = jnp.tile(qk_g_dil[1], N_DIL * DIL_HEADS)[None, :]
    bf = lambda w: w.astype(BF16)
    return (bf(wa), bf(wb), bf(wc), bf(wd), bf(w_gate_a), g_qa[None, :], bf(wqb), g_kva[None, :],
            bf(wkvk), bf(wkvv)), (gmq, gmk, gsq, gsk, gdq, gdk, oq, ok, ov), bounded


def kernel(x, norm1_g, w_in, g_qa, w_qb, g_kva, w_kvb, qk_g_mla, qk_g_sw, qk_g_dil, sinks, rel_bias, w_gate_a, w_gate_b, b_gate, w_branch, w_out, norm2_g, w_gu_dense, w_down_dense, w_router, b_router, w_gu_exp, w_down_exp):
    b, s, d = x.shape
    n = b * s
    depth = norm1_g.shape[0]
    bias = _bias_tables(rel_bias)
    sw_bias = bias[:SW_Q_HEADS]
    dil_bias = [bias[SW_Q_HEADS + g * DIL_HEADS:SW_Q_HEADS + (g + 1) * DIL_HEADS] for g in range(N_DIL)]
    rc, rs1, rs2 = _rope_tables(s)
    x2 = x.reshape(n, d)
    seq3 = lambda a: a.reshape(b, s, a.shape[-1])
    flat = lambda a: a.reshape(n, a.shape[-1])

    for i in range(depth):
        weights, gains, bounded = _layer_weights(w_in[i], g_qa[i], w_qb[i], g_kva[i], w_kvb[i],
                                                 qk_g_mla[i], qk_g_sw[i], qk_g_dil[i], w_gate_a[i])
        wa, wb, wc, wd, wg, gqa, wqb, gkva, wkvk, wkvv = weights
        (sbq, sbk, sbv, mq, mk, mv, swq, swk, swv,
         dq0, dq1, dq2, dk0, dk1, dk2, dv0, dv1, dv2, glow) = _inproj(
            x2, norm1_g[i][None, :], wa, wb, wc, wd, wg, gqa, wqb, gkva, wkvk, wkvv, rc, rs1, rs2,
            *gains, s)

        out_a = _sb_attention(seq3(sbq), seq3(sbk), seq3(sbv))
        out_b = _mla_attention(seq3(mq), seq3(mk), seq3(mv), bounded)
        out_c = _band_attention(seq3(swq), seq3(swk), seq3(swv), sw_bias, sinks[i], 1,
                                SW_Q_HEADS, False, False, BF16)[0]
        dil_o, dil_l = [], []
        for g, (dq, dk, dv) in enumerate(((dq0, dk0, dv0), (dq1, dk1, dv1), (dq2, dk2, dv2))):
            o, l = _band_attention(seq3(dq), seq3(dk), seq3(dv), dil_bias[g], None,
                                   DIL_PATTERNS[g][1], DIL_HEADS, True, True, F32)
            dil_o.append(flat(o))
            dil_l.append(flat(l))

        router = None
        if i % 2 == 1:
            wr = jnp.pad(w_router[i // 2], ((0, 0), (0, LANES - N_EXPERTS)))
            wr_hi = wr.astype(BF16)
            wr_lo = (wr - wr_hi.astype(F32)).astype(BF16)
            br = jnp.pad(b_router[i // 2], (0, LANES - N_EXPERTS), constant_values=NEG)[None, :]
            router = (wr_hi, wr_lo, br)
        merged = _merge(x2, glow, flat(out_a), flat(out_b), flat(out_c), dil_o, dil_l,
                        w_gate_b[i].astype(BF16), b_gate[i][None, :], w_branch[i].astype(BF16),
                        w_out[i].astype(BF16), norm2_g[i][None, :], router)
        if i % 2 == 0:
            x1, h2 = merged
            w_gu = w_gu_dense[i // 2].astype(BF16)
            x2 = _dense_ffn(x1, h2, w_gu[:, :D_FF_DENSE], w_gu[:, D_FF_DENSE:],
                            w_down_dense[i // 2].astype(BF16))
        else:
            x1, h2, wts = merged
            w_gu = w_gu_exp[i // 2].astype(BF16)
            x2 = _moe_ffn(x1, h2, wts, w_gu[:, :, :D_FF_EXPERT], w_gu[:, :, D_FF_EXPERT:],
                          w_down_exp[i // 2].astype(BF16))
    return x2.reshape(b, s, d)
```

```python
import functools
import math

import jax
import jax.numpy as jnp
import numpy as np
from jax import lax
from jax.experimental import pallas as pl
from jax.experimental.pallas import tpu as pltpu

F32 = jnp.float32
BF16 = jnp.bfloat16

D_MODEL = 1024
BLOCK = 128
EPS = 1e-6
SB_HEADS, SB_DIM = 4, 64
MLA_HEADS, MLA_Q_RANK, MLA_KV_RANK = 4, 256, 256
MLA_NOPE, MLA_ROPE, MLA_V = 64, 32, 64
MLA_QK = MLA_NOPE + MLA_ROPE
ROPE_THETA = 10000.0
SW_Q_HEADS, SW_KV_HEADS, SW_DIM = 8, 2, 32
DIL_PATTERNS = ((128, 1), (512, 4), (2048, 16))
DIL_HEADS, DIL_DIM = 4, 32
N_DIL = len(DIL_PATTERNS)
REL_BUCKETS, REL_MAX_DIST = 32, 2048
GATE_RANK, N_BRANCH = 128, 4
D_FF_DENSE = 2048
N_EXPERTS, TOP_K, D_FF_EXPERT = 8, 2, 768

SB_W = SB_HEADS * SB_DIM
SW_W = SW_Q_HEADS * SW_DIM
SW_KV_W = SW_KV_HEADS * SW_DIM
DIL_GW = DIL_HEADS * DIL_DIM
DIL_W = N_DIL * DIL_GW
IN_SIZES = (SB_W, SB_W, SB_W, MLA_Q_RANK, MLA_KV_RANK, MLA_ROPE, SW_W, SW_KV_W, SW_KV_W, DIL_W, DIL_W, DIL_W)
BR_SIZES = (SB_W, MLA_HEADS * MLA_V, SW_W, DIL_GW)

LANES = 128
MLA_PAD = LANES
NEG = -1e30
VMEM_LIMIT = 56 * 1024 * 1024

ROW_TILE = 1024
MERGE_COL_CHUNK = 256
MERGE_ROW_TILE = 1024
MOE_ROW_TILE = 1024
MOE_EXPERTS_PER_STEP = 2
ATT_TILE = 256
MLA_Q_TILE = 2048
MLA_K_TILE = 512
SB_LOG_FLOOR = -152.0
MLA_BOUND_MARGIN = 1.03
MLA_MAX_SHIFT = 60.0
BAND_PLAN = {1: (16, 1), 4: (8, 2), 16: (2, 8)}


def _dot(a, b):
    return jnp.dot(a, b, preferred_element_type=F32)


def _dot_nt(a, b):
    return lax.dot_general(a, b, (((1,), (1,)), ((), ())), preferred_element_type=F32)


def _dot_tn(a, b):
    return lax.dot_general(a, b, (((0,), (0,)), ((), ())), preferred_element_type=F32)


def _params(sem):
    return pltpu.CompilerParams(dimension_semantics=sem, vmem_limit_bytes=VMEM_LIMIT)


def _full(shape):
    nd = len(shape)
    return pl.BlockSpec(shape, lambda *_: (0,) * nd)


def _rms(v, g):
    return v * lax.rsqrt(jnp.mean(v * v, axis=-1, keepdims=True) + EPS) * g


def _split_bf16(v):
    hi = v.astype(BF16)
    lo = (v - hi.astype(F32)).astype(BF16)
    return hi, lo


def _group_matrix(group):
    r = (lax.broadcasted_iota(jnp.int32, (2 * LANES, LANES), 0) & (LANES - 1)) // group
    c = lax.broadcasted_iota(jnp.int32, (2 * LANES, LANES), 1) // group
    return jnp.where(r == c, 1.0, 0.0).astype(BF16)


def _group_norm_slab(xs, gmat, inv_n):
    hi, lo = _split_bf16(xs * xs)
    ss = _dot(jnp.concatenate([hi, lo], axis=1), gmat)
    return xs * lax.rsqrt(ss * inv_n + EPS)


def _inproj_kernel(x_ref, g1_ref, wa_ref, wb_ref, wc_ref, wd_ref, wg_ref,
                   gqa_ref, wqb_ref, gkva_ref, wkvk_ref, wkvv_ref,
                   rc_ref, rs1_ref, rs2_ref,
                   gmq_ref, gmk_ref, gsq_ref, gsk_ref, gdq_ref, gdk_ref, oq_ref, ok_ref, ov_ref,
                   sbq, sbk, sbv, mq, mk, mv, swq, swk, swv,
                   dq0, dq1, dq2, dk0, dk1, dk2, dv0, dv1, dv2, glow):
    x = x_ref[...]
    h = _rms(x, g1_ref[...]).astype(BF16)

    pb = _dot(h, wb_ref[...])
    cq = pb[:, :MLA_Q_RANK]
    ckv = pb[:, MLA_Q_RANK:MLA_Q_RANK + MLA_KV_RANK]
    kpe = pb[:, MLA_Q_RANK + MLA_KV_RANK:]
    half = MLA_ROPE // 2

    def rope(v, reps):
        tile = lambda t: jnp.concatenate([t] * reps, axis=1) if reps > 1 else t
        width = reps * LANES
        return (v * tile(rc_ref[...]) + pltpu.roll(v, half, 1) * tile(rs1_ref[...])
                + pltpu.roll(v, width - half, 1) * tile(rs2_ref[...]))

    ones = _group_matrix(LANES)
    q = rope(_dot(_rms(cq, gqa_ref[...]).astype(BF16), wqb_ref[...]), MLA_HEADS)
    kvn = _rms(ckv, gkva_ref[...]).astype(BF16)
    k = _dot(kvn, wkvk_ref[...]) + jnp.concatenate([rope(kpe, 1)] * MLA_HEADS, axis=1)
    mv[...] = (_dot(kvn, wkvv_ref[...]) + ov_ref[...]).astype(BF16)
    gmq = gmq_ref[...]
    gmk = gmk_ref[...]
    oq = oq_ref[...]
    ok = ok_ref[...]
    for s in range(MLA_HEADS):
        sl = slice(s * LANES, (s + 1) * LANES)
        mq[:, sl] = (_group_norm_slab(q[:, sl], ones, 1.0 / MLA_QK) * gmq[:, sl] + oq[:, sl]).astype(BF16)
        mk[:, sl] = (_group_norm_slab(k[:, sl], ones, 1.0 / MLA_QK) * gmk[:, sl] + ok[:, sl]).astype(BF16)

    pa = _dot(h, wa_ref[...])
    sbq[...] = (pa[:, :SB_W] * math.log2(math.e)).astype(BF16)
    sbk[...] = pa[:, SB_W:2 * SB_W].astype(BF16)
    sbv[...] = pa[:, 2 * SB_W:].astype(BF16)

    g32 = _group_matrix(SW_DIM)
    pc = _dot(h, wc_ref[...])
    gsq = gsq_ref[...]
    gsk = gsk_ref[...]
    for s in range(SW_W // LANES):
        sl = slice(s * LANES, (s + 1) * LANES)
        ks = slice(SW_W + s * LANES, SW_W + (s + 1) * LANES)
        swq[:, sl] = (_group_norm_slab(pc[:, sl], g32, 1.0 / SW_DIM) * gsq[:, sl]).astype(BF16)
        swk[:, sl] = (_group_norm_slab(pc[:, ks], g32, 1.0 / SW_DIM) * gsk[:, sl]).astype(BF16)
    swv[...] = pc[:, 2 * SW_W:].astype(BF16)

    pd = _dot(h, wd_ref[...])
    gdq = gdq_ref[...]
    gdk = gdk_ref[...]
    for g, (q_out, k_out, v_out) in enumerate(((dq0, dk0, dv0), (dq1, dk1, dv1), (dq2, dk2, dv2))):
        sl = slice(g * LANES, (g + 1) * LANES)
        ks = slice(DIL_W + g * LANES, DIL_W + (g + 1) * LANES)
        vs = slice(2 * DIL_W + g * LANES, 2 * DIL_W + (g + 1) * LANES)
        q_out[...] = _group_norm_slab(pd[:, sl], g32, 1.0 / DIL_DIM) * gdq[:, sl]
        k_out[...] = _group_norm_slab(pd[:, ks], g32, 1.0 / DIL_DIM) * gdk[:, sl]
        v_out[...] = pd[:, vs]

    glow[...] = _dot(h, wg_ref[...]).astype(BF16)


def _inproj(x2, g1, wa, wb, wc, wd, wg, gqa, wqb, gkva, wkvk, wkvv, rc, rs1, rs2,
            gmq, gmk, gsq, gsk, gdq, gdk, oq, ok, ov, seq):
    n = x2.shape[0]
    tm = ROW_TILE
    n_seq_tiles = seq // tm
    row = lambda w: pl.BlockSpec((tm, w), lambda i: (i, 0))
    pos = lambda w: pl.BlockSpec((tm, w), lambda i: (i % n_seq_tiles, 0))
    weights = (g1, wa, wb, wc, wd, wg, gqa, wqb, gkva, wkvk, wkvv)
    gains = (gmq, gmk, gsq, gsk, gdq, gdk, oq, ok, ov)
    outs = [(SB_W, BF16)] * 3 + [(MLA_HEADS * MLA_PAD, BF16)] * 3 + [(SW_W, BF16)] * 3 \
        + [(DIL_GW, F32)] * 9 + [(GATE_RANK, BF16)]
    return pl.pallas_call(
        _inproj_kernel,
        grid=(n // tm,),
        in_specs=[row(D_MODEL)] + [_full(w.shape) for w in weights] + [pos(LANES)] * 3
        + [_full(g.shape) for g in gains],
        out_specs=[row(w) for w, _ in outs],
        out_shape=[jax.ShapeDtypeStruct((n, w), dt) for w, dt in outs],
        compiler_params=_params(("parallel",)),
        name="inproj",
    )(x2, *weights, rc, rs1, rs2, *gains)


def _sb_kernel(q_ref, k_ref, v_ref, o_ref, acc_ref, car_ref):
    t = ATT_TILE
    pairs = SB_HEADS // 2
    i = pl.program_id(1)
    lane = lax.broadcasted_iota(jnp.int32, (t, LANES), 1)
    qs = []
    for p in range(pairs):
        q = q_ref[0, :, p * LANES:(p + 1) * LANES]
        zero = jnp.zeros_like(q)
        qs.append(jnp.concatenate([jnp.where(lane < SB_DIM, q, zero), jnp.where(lane >= SB_DIM, q, zero)], axis=0))
    r = lax.broadcasted_iota(jnp.int32, (t, t), 0)
    c = lax.broadcasted_iota(jnp.int32, (t, t), 1)
    suffix = jnp.where(r >= c, 1.0, 0.0).astype(BF16)
    r2 = lax.broadcasted_iota(jnp.int32, (2 * t, t), 0) & (t - 1)
    c2 = lax.broadcasted_iota(jnp.int32, (2 * t, t), 1)
    strict = c2 < r2

    acc_ref[...] = jnp.zeros_like(acc_ref)
    car_ref[...] = jnp.zeros_like(car_ref)

    def blocks(j, count, diagonal):
        kbs, vbs = [], []
        for d in range(count):
            off = pl.multiple_of((j - d) * t, t)
            kbs.append(k_ref[0, pl.ds(off, t), :])
            vbs.append(v_ref[0, pl.ds(off, t), :])
        chains = [(p, d) for p in range(pairs) for d in range(count)]
        zs = [_dot_nt(qs[p], kbs[d][:, p * LANES:(p + 1) * LANES]) for p, d in chains]
        lks = []
        for z, (_, d) in zip(zs, chains):
            lk = -(jnp.maximum(z, 0.0) + jnp.log2(1.0 + jnp.exp2(-jnp.abs(z))))
            lks.append(jnp.where(strict, lk, 0.0) if diagonal and d == 0 else lk)
        withins = [_dot(lk.astype(BF16), suffix) for lk in lks]
        cars = [car_ref[p] for p in range(pairs)]
        weights = []
        for c, (p, d) in enumerate(chains):
            a = jnp.exp2(jnp.minimum(zs[c] + withins[c], 0.0) + cars[p])
            weights.append((jnp.where(strict, a, 0.0) if diagonal and d == 0 else a).astype(BF16))
            cars[p] = cars[p] + withins[c][:, :1]
        for p in range(pairs):
            car_ref[p] = cars[p]
        for c, (p, d) in enumerate(chains):
            acc_ref[p] += _dot(weights[c], vbs[d][:, p * LANES:(p + 1) * LANES])
        return jnp.max(functools.reduce(jnp.maximum, cars)) > SB_LOG_FLOOR

    first = jnp.minimum(i + 1, 2)
    alive = lax.cond(i >= 1, lambda: blocks(i, 2, True), lambda: blocks(i, 1, True))

    def live(state):
        s, alive = state
        return jnp.logical_and(s <= i, alive)

    def body(state):
        s, _ = state
        return s + 1, blocks(i - s, 1, False)

    lax.while_loop(live, body, (first, alive))
    for p in range(pairs):
        o_ref[0, :, p * LANES:(p + 1) * LANES] = jnp.where(
            lane < SB_DIM, acc_ref[p, :t], acc_ref[p, t:]).astype(o_ref.dtype)


def _sb_attention(q, k, v):
    b, s, _ = q.shape
    t = ATT_TILE
    pairs = SB_HEADS // 2
    return pl.pallas_call(
        _sb_kernel,
        grid=(b, s // t),
        in_specs=[pl.BlockSpec((1, t, SB_W), lambda bi, i: (bi, i, 0)),
                  pl.BlockSpec((1, s, SB_W), lambda bi, i: (bi, 0, 0)),
                  pl.BlockSpec((1, s, SB_W), lambda bi, i: (bi, 0, 0))],
        out_specs=pl.BlockSpec((1, t, SB_W), lambda bi, i: (bi, i, 0)),
        out_shape=jax.ShapeDtypeStruct((b, s, SB_W), BF16),
        scratch_shapes=[pltpu.VMEM((pairs, 2 * t, LANES), F32), pltpu.VMEM((pairs, 2 * t, 1), F32)],
        compiler_params=_params(("parallel", "arbitrary")),
        name="stick_breaking",
    )(q, k, v)


def _mla_kernel(q_ref, k_ref, v_ref, o_ref, acc_ref, *stat_refs, online):
    tq, tk = MLA_Q_TILE, MLA_K_TILE
    sub = tq // tk
    i = pl.program_id(2)
    r = lax.broadcasted_iota(jnp.int32, (tk, tk), 0)
    c = lax.broadcasted_iota(jnp.int32, (tk, tk), 1)
    causal = c <= r
    lane = lax.broadcasted_iota(jnp.int32, (tq, LANES), 1)
    reps = tk // LANES

    acc_ref[...] = jnp.zeros_like(acc_ref)
    if online:
        m_ref = stat_refs[0]
        m_ref[...] = jnp.full_like(m_ref, NEG)

    def block(j, rows, diagonal):
        off = pl.multiple_of(j * tk, tk)
        kb = k_ref[0, pl.ds(off, tk), :]
        vb = v_ref[0, pl.ds(off, tk), :]
        for h in range(2):
            sl = slice(h * MLA_PAD, (h + 1) * MLA_PAD)
            s = _dot_nt(q_ref[0, rows, sl], kb[:, sl])
            if diagonal:
                s = jnp.where(causal, s, NEG)
            if online:
                m_old = m_ref[h, rows, :]
                m_new = jnp.maximum(m_old, jnp.max(s, axis=-1, keepdims=True))
                p = jnp.exp2(s - jnp.tile(m_new, (1, reps)))
                acc_ref[h, rows, :] = (jnp.exp2(m_old - m_new) * acc_ref[h, rows, :]
                                       + _dot(p.astype(BF16), vb[:, sl]))
                m_ref[h, rows, :] = m_new
            else:
                acc_ref[h, rows, :] += _dot(jnp.exp2(s).astype(BF16), vb[:, sl])

    for a in range(sub):
        rows = slice(a * tk, (a + 1) * tk)
        for d in range(a + 1):
            block(i * sub + d, rows, d == a)

    def body(j, carry):
        block(j, slice(0, tq), False)
        return carry

    lax.fori_loop(0, i * sub, body, 0)
    a0, a1 = acc_ref[0], acc_ref[1]
    half = LANES // 2
    o0 = a0 / pltpu.roll(a0, half, 1)
    o1 = pltpu.roll(a1, half, 1) / a1
    o_ref[0] = jnp.where(lane < MLA_V, o0, o1).astype(o_ref.dtype)


def _mla_attention(q, k, v, bounded):
    b, s, _ = q.shape
    t = MLA_Q_TILE
    stat = pltpu.VMEM((2, t, LANES), F32)

    def call(online):
        return pl.pallas_call(
            functools.partial(_mla_kernel, online=online),
            grid=(b, MLA_HEADS // 2, s // t),
            in_specs=[pl.BlockSpec((1, t, 2 * MLA_PAD), lambda bi, p, i: (bi, i, p)),
                      pl.BlockSpec((1, s, 2 * MLA_PAD), lambda bi, p, i: (bi, 0, p)),
                      pl.BlockSpec((1, s, 2 * MLA_PAD), lambda bi, p, i: (bi, 0, p))],
            out_specs=pl.BlockSpec((1, t, LANES), lambda bi, p, i: (bi, i, p)),
            out_shape=jax.ShapeDtypeStruct((b, s, MLA_HEADS * MLA_V), BF16),
            scratch_shapes=[stat, stat] if online else [stat],
            compiler_params=_params(("parallel", "parallel", "arbitrary")),
            name="latent_attention_online" if online else "latent_attention",
        )

    return lax.cond(bounded, call(False), call(True), q, k, v)


def _band_kernel(*refs, heads, dil, blocks, fan, with_sinks, with_lse):
    q_ref, kp_ref, kc_ref, vp_ref, vc_ref, bias_ref, first_ref = refs[:7]
    rest = refs[7:]
    if with_sinks:
        sink_ref, rest = rest[0], rest[1:]
    o_ref = rest[0]
    width = heads * DIL_DIM
    n = pl.program_id(1)
    group = lax.broadcasted_iota(jnp.int32, (BLOCK, LANES), 1) // DIL_DIM
    bias = bias_ref[...]
    first_bias = jnp.where(n > 0, bias, first_ref[...])

    def rows(r, blk):
        start = r + blk * (BLOCK * dil)
        return pl.ds(start, BLOCK, stride=dil) if dil > 1 else pl.ds(start, BLOCK)

    def load(ref, r, blk):
        return ref[0, rows(r, blk), :].astype(BF16)

    def streams(r0):
        slabs = width // LANES
        hs = heads // slabs
        cols = hs * BLOCK
        idx, vs, ss, ps, ls, ms = [], [], [], [], [], []
        for r, j in [(r0 + dr, j) for dr in range(fan) for j in range(blocks)]:
            q_all = load(q_ref, r, j)
            k_all = jnp.concatenate([load(kp_ref, r, 0) if j == 0 else load(kc_ref, r, j - 1),
                                     load(kc_ref, r, j)], axis=0)
            v_all = jnp.concatenate([load(vp_ref, r, 0) if j == 0 else load(vc_ref, r, j - 1),
                                     load(vc_ref, r, j)], axis=0)
            for g in range(slabs):
                lanes = slice(g * LANES, (g + 1) * LANES)
                q = q_all[:, lanes]
                zero = jnp.zeros_like(q)
                qs = jnp.concatenate([jnp.where(group == h, q, zero) for h in range(hs)], axis=0)
                vs.append(v_all[:, lanes])
                table = first_bias if j == 0 else bias
                ss.append(_dot_nt(k_all[:, lanes], qs) + table[:, g * cols:(g + 1) * cols])
                idx.append((rows(r, j), lanes, g))
        for s, (_, _, g) in zip(ss, idx):
            m = jnp.max(s, axis=0, keepdims=True)
            if with_sinks:
                sink = sink_ref[:, g * cols:(g + 1) * cols]
                m = jnp.maximum(m, sink)
            p = jnp.exp2(s - m)
            l = jnp.sum(p, axis=0, keepdims=True)
            if with_sinks:
                l = l + jnp.exp2(sink - m)
            ps.append(p.astype(BF16))
            ls.append(l)
            ms.append(m)
        os_ = [_dot_tn(v, p) * (1.0 / l) for v, p, l in zip(vs, ps, ls)]
        for c, o in enumerate(os_):
            rws, lanes, _ = idx[c]
            out = jnp.concatenate([o[h * DIL_DIM:(h + 1) * DIL_DIM, h * BLOCK:(h + 1) * BLOCK]
                                   for h in range(hs)], axis=0)
            o_ref[0, rws, lanes] = out.T.astype(o_ref.dtype)
            if with_lse:
                lg = (ms[c] + jnp.log2(ls[c])) * math.log(2.0)
                lse = jnp.concatenate([jnp.broadcast_to(lg[:, h * BLOCK:(h + 1) * BLOCK], (DIL_DIM, BLOCK))
                                       for h in range(hs)], axis=0)
                rest[1][0, rws, lanes] = lse.T

    if dil == fan:
        streams(0)
    else:
        def body(t, carry):
            streams(t * fan)
            return carry
        lax.fori_loop(0, dil // fan, body, 0)


def _band_attention(q, k, v, bias, sinks, dil, heads, inclusive, with_lse, out_dtype):
    b, s, width = q.shape
    blocks, fan = BAND_PLAN[dil]
    blocks = blocks // (width // LANES)
    step = blocks * BLOCK * dil
    qi = np.arange(BLOCK)[None, :, None]
    kj = np.arange(2 * BLOCK)[None, None, :]
    window = ((kj >= qi) if inclusive else (kj > qi)) & (kj <= qi + BLOCK)
    table = lambda mask: jnp.where(mask, bias * math.log2(math.e), NEG).transpose(2, 0, 1).reshape(
        2 * BLOCK, heads * BLOCK)
    tables = (table(window), table(window & (kj >= BLOCK)))
    cur = pl.BlockSpec((1, step, width), lambda bi, n: (bi, n, 0))
    prev = pl.BlockSpec((1, BLOCK * dil, width), lambda bi, n: (bi, jnp.maximum(n * blocks - 1, 0), 0))
    in_specs = [cur, prev, cur, prev, cur] + [_full(t.shape) for t in tables]
    args = [q, k, k, v, v, *tables]
    if sinks is not None:
        sink_cols = jnp.repeat(sinks * math.log2(math.e), BLOCK)[None, :]
        in_specs.append(_full(sink_cols.shape))
        args.append(sink_cols)
    out_shape = [jax.ShapeDtypeStruct((b, s, width), out_dtype)]
    out_specs = [cur]
    if with_lse:
        out_shape.append(jax.ShapeDtypeStruct((b, s, width), F32))
        out_specs.append(cur)
    return pl.pallas_call(
        functools.partial(_band_kernel, heads=heads, dil=dil, blocks=blocks, fan=fan,
                          with_sinks=sinks is not None, with_lse=with_lse),
        grid=(b, s // step),
        in_specs=in_specs,
        out_specs=out_specs,
        out_shape=out_shape,
        compiler_params=_params(("parallel", "arbitrary")),
        name="band_attention",
    )(*args)


def _bias_kernel(rb_ref, bkt_ref, o_ref, *, head_patterns):
    for head, pat in enumerate(head_patterns):
        bkt = bkt_ref[pat]
        acc = jnp.zeros(bkt.shape, F32)
        for bucket in range(REL_BUCKETS):
            acc = jnp.where(bkt == bucket, rb_ref[bucket, head], acc)
        o_ref[head] = acc


def _t5_bucket(dist):
    max_exact = REL_BUCKETS // 2
    d = np.maximum(dist, 1).astype(np.float32)
    large = max_exact + (np.log(d / np.float32(max_exact)) / np.float32(math.log(REL_MAX_DIST / max_exact))
                         * np.float32(REL_BUCKETS - max_exact)).astype(np.int32)
    large = np.minimum(large, REL_BUCKETS - 1)
    return np.where(dist < max_exact, dist, large).astype(np.int32)


def _bias_tables(rel_bias):
    dist = (BLOCK + np.arange(BLOCK))[:, None] - np.arange(2 * BLOCK)[None, :]
    dist = np.maximum(dist, 0)
    buckets = jnp.asarray(np.stack([_t5_bucket(dist * dil) for _, dil in DIL_PATTERNS]))
    head_patterns = (0,) * SW_Q_HEADS + tuple(g for g in range(N_DIL) for _ in range(DIL_HEADS))
    n_heads = len(head_patterns)
    return pl.pallas_call(
        functools.partial(_bias_kernel, head_patterns=head_patterns),
        in_specs=[pl.BlockSpec(memory_space=pltpu.SMEM), _full(buckets.shape)],
        out_specs=_full((n_heads, BLOCK, 2 * BLOCK)),
        out_shape=jax.ShapeDtypeStruct((n_heads, BLOCK, 2 * BLOCK), F32),
        grid=(1,),
        name="rel_bias",
    )(rel_bias, buckets)


def _merge_kernel(*refs, with_router):
    (x_ref, glow_ref, oa_ref, ob_ref, oc_ref, d0_ref, d1_ref, d2_ref, l0_ref, l1_ref, l2_ref,
     wgb_ref, bg_ref, wbr_ref, wout_ref, g2_ref) = refs[:16]
    rest = refs[16:]
    if with_router:
        wr_ref, br_ref = rest[:2]
        rest = rest[2:]
    x1_ref, h2_ref = rest[:2]

    l0, l1, l2 = l0_ref[...], l1_ref[...], l2_ref[...]
    mx = jnp.maximum(jnp.maximum(l0, l1), l2)
    e0, e1, e2 = jnp.exp(l0 - mx), jnp.exp(l1 - mx), jnp.exp(l2 - mx)
    od = (e0 * d0_ref[...] + e1 * d1_ref[...] + e2 * d2_ref[...]) / (e0 + e1 + e2)

    glow = glow_ref[...]
    branches = (oa_ref[...], ob_ref[...], oc_ref[...], od.astype(BF16))
    chunk = MERGE_COL_CHUNK
    ys = []
    for c in range(D_MODEL // chunk):
        y = None
        row = 0
        for i, o in enumerate(branches):
            cols = slice(i * D_MODEL + c * chunk, i * D_MODEL + (c + 1) * chunk)
            pre = _dot(glow, wgb_ref[:, cols]) + bg_ref[:, cols]
            gate = 0.5 * jnp.tanh(0.5 * pre) + 0.5
            term = gate * _dot(o, wbr_ref[row:row + BR_SIZES[i], c * chunk:(c + 1) * chunk])
            y = term if y is None else y + term
            row += BR_SIZES[i]
        ys.append(y.astype(BF16))
    x1 = x_ref[...] + _dot(jnp.concatenate(ys, axis=1), wout_ref[...])
    x1_ref[...] = x1
    h2 = _rms(x1, g2_ref[...])
    h2_ref[...] = h2.astype(BF16)

    if with_router:
        wts_ref = rest[2]
        hi, lo = _split_bf16(h2)
        both = _dot(hi, wr_ref[...])
        lg = both[:, :LANES] + both[:, LANES:] + _dot(lo, wr_ref[:, :LANES]) + br_ref[...]
        lane = lax.broadcasted_iota(jnp.int32, lg.shape, 1).astype(F32)
        big = float(LANES)
        m1 = jnp.max(lg, axis=-1, keepdims=True)
        i1 = jnp.min(jnp.where(lg == m1, lane, big), axis=-1, keepdims=True)
        lg2 = jnp.where(lane == i1, NEG, lg)
        m2 = jnp.max(lg2, axis=-1, keepdims=True)
        i2 = jnp.min(jnp.where(lg2 == m2, lane, big), axis=-1, keepdims=True)
        e = jnp.exp(m2 - m1)
        w1 = 1.0 / (1.0 + e)
        w2 = e / (1.0 + e)
        wts_ref[...] = jnp.where(lane == i1, w1, 0.0) + jnp.where(lane == i2, w2, 0.0)


def _merge(x2, glow, oa, ob, oc, dil_o, dil_l, wgb, bg, wbr, wout, g2, router):
    n = x2.shape[0]
    tm = MERGE_ROW_TILE
    row = lambda w: pl.BlockSpec((tm, w), lambda i: (i, 0))
    acts = [x2, glow, oa, ob, oc, *dil_o, *dil_l]
    weights = [wgb, bg, wbr, wout, g2] + (list(router) if router is not None else [])
    out_shape = [jax.ShapeDtypeStruct((n, D_MODEL), F32), jax.ShapeDtypeStruct((n, D_MODEL), BF16)]
    out_specs = [row(D_MODEL), row(D_MODEL)]
    if router is not None:
        out_shape.append(jax.ShapeDtypeStruct((n, LANES), F32))
        out_specs.append(row(LANES))
    return pl.pallas_call(
        functools.partial(_merge_kernel, with_router=router is not None),
        grid=(n // tm,),
        in_specs=[row(a.shape[1]) for a in acts] + [_full(w.shape) for w in weights],
        out_specs=out_specs,
        out_shape=out_shape,
        compiler_params=_params(("parallel",)),
        name="merge",
    )(*acts, *weights)


FF_CHUNK = 512


def _dense_ffn_kernel(x1_ref, h2_ref, wg_ref, wu_ref, wd_ref, o_ref):
    h2 = h2_ref[...]
    acc = x1_ref[...]
    for c in range(D_FF_DENSE // FF_CHUNK):
        cols = slice(c * FF_CHUNK, (c + 1) * FF_CHUNK)
        g = _dot(h2, wg_ref[:, cols])
        u = _dot(h2, wu_ref[:, cols])
        act = (g / (1.0 + jnp.exp(-g)) * u).astype(BF16)
        acc = acc + _dot(act, wd_ref[cols, :])
    o_ref[...] = acc


def _dense_ffn(x1, h2, w_gu, wd):
    n = x1.shape[0]
    tm = ROW_TILE
    row = pl.BlockSpec((tm, D_MODEL), lambda i: (i, 0))
    half = lambda c: pl.BlockSpec((D_MODEL, D_FF_DENSE), lambda i: (0, c))
    return pl.pallas_call(
        _dense_ffn_kernel,
        grid=(n // tm,),
        in_specs=[row, row, half(0), half(1), _full(wd.shape)],
        out_specs=row,
        out_shape=jax.ShapeDtypeStruct((n, D_MODEL), F32),
        compiler_params=_params(("parallel",)),
        name="dense_ffn",
    )(x1, h2, w_gu, w_gu, wd)


def _moe_kernel(x1_ref, h2_ref, wts_ref, wg_ref, wu_ref, wd_ref, o_ref):
    step = pl.program_id(1)

    @pl.when(step == 0)
    def _():
        o_ref[...] = x1_ref[...]

    wts = wts_ref[...]
    lane = lax.broadcasted_iota(jnp.int32, wts.shape, 1)
    h2 = h2_ref[...]
    y = None
    for j in range(MOE_EXPERTS_PER_STEP):
        e = step * MOE_EXPERTS_PER_STEP + j
        w_e = jnp.sum(jnp.where(lane == e, wts, 0.0), axis=-1, keepdims=True)
        g = _dot(h2, wg_ref[j])
        u = _dot(h2, wu_ref[j])
        act = (g / (1.0 + jnp.exp(-g)) * u).astype(BF16)
        part = w_e * _dot(act, wd_ref[j])
        y = part if y is None else y + part
    o_ref[...] += y


def _moe_ffn(x1, h2, wts, w_gu, wd):
    n = x1.shape[0]
    tm = MOE_ROW_TILE
    ne = MOE_EXPERTS_PER_STEP
    row = lambda w: pl.BlockSpec((tm, w), lambda i, e: (i, 0))
    return pl.pallas_call(
        _moe_kernel,
        grid=(n // tm, N_EXPERTS // ne),
        in_specs=[row(D_MODEL), row(D_MODEL), row(LANES),
                  pl.BlockSpec((ne, D_MODEL, D_FF_EXPERT), lambda i, e: (e, 0, 0)),
                  pl.BlockSpec((ne, D_MODEL, D_FF_EXPERT), lambda i, e: (e, 0, 1)),
                  pl.BlockSpec((ne, D_FF_EXPERT, D_MODEL), lambda i, e: (e, 0, 0))],
        out_specs=row(D_MODEL),
        out_shape=jax.ShapeDtypeStruct((n, D_MODEL), F32),
        compiler_params=_params(("parallel", "arbitrary")),
        name="moe_ffn",
    )(x1, h2, wts, w_gu, w_gu, wd)


def _rope_tables(seq):
    half = MLA_ROPE // 2
    inv = ROPE_THETA ** (-jnp.arange(half, dtype=F32) / half)
    ang = jnp.arange(seq, dtype=F32)[:, None] * inv[None, :]
    cos, sin = jnp.cos(ang), jnp.sin(ang)
    zeros = lambda w: jnp.zeros((seq, w), F32)
    tail = LANES - MLA_QK
    rc = jnp.concatenate([jnp.ones((seq, MLA_NOPE), F32), cos, cos, zeros(tail)], axis=1)
    rs1 = jnp.concatenate([zeros(MLA_NOPE + half), sin, zeros(tail)], axis=1)
    rs2 = jnp.concatenate([zeros(MLA_NOPE), -sin, zeros(half + tail)], axis=1)
    return rc, rs1, rs2


def _pad_heads(w, heads, dim):
    rows = w.shape[0]
    w = w.reshape(rows, heads, dim)
    return jnp.pad(w, ((0, 0), (0, 0), (0, LANES - dim))).reshape(rows, heads * LANES)


def _layer_weights(w_in, g_qa, w_qb, g_kva, w_kvb, qk_g_mla, qk_g_sw, qk_g_dil, w_gate_a):
    offs = np.concatenate([[0], np.cumsum(IN_SIZES)]).tolist()
    cols = [w_in[:, offs[j]:offs[j + 1]] for j in range(len(IN_SIZES))]
    a_q, a_k, a_v, b_cq, b_ckv, b_kpe, c_q, c_k, c_v, d_q, d_k, d_v = cols
    wa = jnp.concatenate([a_q * (SB_DIM ** -0.5), a_k, a_v], axis=1)
    kpe_pad = jnp.pad(b_kpe, ((0, 0), (MLA_NOPE, LANES - MLA_QK)))
    wb = jnp.concatenate([b_cq, b_ckv, kpe_pad], axis=1)
    rep = SW_Q_HEADS // SW_KV_HEADS
    expand = lambda w: jnp.repeat(w.reshape(D_MODEL, SW_KV_HEADS, SW_DIM), rep, axis=1).reshape(D_MODEL, SW_W)
    wc = jnp.concatenate([c_q, expand(c_k), expand(c_v)], axis=1)
    wd = jnp.concatenate([d_q, d_k, d_v], axis=1)
    wqb = _pad_heads(w_qb, MLA_HEADS, MLA_QK)
    kvb = w_kvb.reshape(MLA_KV_RANK, MLA_HEADS, MLA_NOPE + MLA_V)
    wkvk = _pad_heads(kvb[:, :, :MLA_NOPE].reshape(MLA_KV_RANK, -1), MLA_HEADS, MLA_NOPE)
    wkvv = _pad_heads(kvb[:, :, MLA_NOPE:].reshape(MLA_KV_RANK, -1), MLA_HEADS, MLA_V)
    pad_gain = lambda g: jnp.tile(jnp.pad(g, (0, LANES - MLA_QK)), MLA_HEADS)[None, :]
    gmq = pad_gain(qk_g_mla[0]) * (MLA_QK ** -0.5 * math.log2(math.e))
    gmk = pad_gain(qk_g_mla[1])
    bound = jnp.max(jnp.abs(qk_g_mla[0] * qk_g_mla[1])) * (MLA_QK ** 0.5 * math.log2(math.e))
    shift = (bound * MLA_BOUND_MARGIN).astype(BF16).astype(F32)
    bounded = shift < MLA_MAX_SHIFT
    spare = lambda lo, hi: jnp.tile(jnp.pad(jnp.ones((hi - lo,), F32), (lo, LANES - hi)), MLA_HEADS)[None, :]
    oq = jnp.where(bounded, -shift, 0.0) * spare(MLA_QK, MLA_QK + 1)
    ok = spare(MLA_QK, MLA_QK + 1)
    ov = spare(MLA_V, LANES)
    gsq = jnp.tile(qk_g_sw[0], SW_Q_HEADS)[None, :] * (SW_DIM ** -0.5 * math.log2(math.e))
    gsk = jnp.tile(qk_g_sw[1], SW_Q_HEADS)[None, :]
    gdq = jnp.tile(qk_g_dil[0], N_DIL * DIL_HEADS)[None, :] * (DIL_DIM ** -0.5 * math.log2(math.e))
    gdk = jnp.tile(qk_g_dil[1], N_DIL * DIL_HEADS)[None, :]
    bf = lambda w: w.astype(BF16)
    return (bf(wa), bf(wb), bf(wc), bf(wd), bf(w_gate_a), g_qa[None, :], bf(wqb), g_kva[None, :],
            bf(wkvk), bf(wkvv)), (gmq, gmk, gsq, gsk, gdq, gdk, oq, ok, ov), bounded


def kernel(x, norm1_g, w_in, g_qa, w_qb, g_kva, w_kvb, qk_g_mla, qk_g_sw, qk_g_dil, sinks, rel_bias, w_gate_a, w_gate_b, b_gate, w_branch, w_out, norm2_g, w_gu_dense, w_down_dense, w_router, b_router, w_gu_exp, w_down_exp):
    b, s, d = x.shape
    n = b * s
    depth = norm1_g.shape[0]
    bias = _bias_tables(rel_bias)
    sw_bias = bias[:SW_Q_HEADS]
    dil_bias = [bias[SW_Q_HEADS + g * DIL_HEADS:SW_Q_HEADS + (g + 1) * DIL_HEADS] for g in range(N_DIL)]
    rc, rs1, rs2 = _rope_tables(s)
    x2 = x.reshape(n, d)
    seq3 = lambda a: a.reshape(b, s, a.shape[-1])
    flat = lambda a: a.reshape(n, a.shape[-1])

    for i in range(depth):
        weights, gains, bounded = _layer_weights(w_in[i], g_qa[i], w_qb[i], g_kva[i], w_kvb[i],
                                                 qk_g_mla[i], qk_g_sw[i], qk_g_dil[i], w_gate_a[i])
        wa, wb, wc, wd, wg, gqa, wqb, gkva, wkvk, wkvv = weights
        (sbq, sbk, sbv, mq, mk, mv, swq, swk, swv,
         dq0, dq1, dq2, dk0, dk1, dk2, dv0, dv1, dv2, glow) = _inproj(
            x2, norm1_g[i][None, :], wa, wb, wc, wd, wg, gqa, wqb, gkva, wkvk, wkvv, rc, rs1, rs2,
            *gains, s)

        out_a = _sb_attention(seq3(sbq), seq3(sbk), seq3(sbv))
        out_b = _mla_attention(seq3(mq), seq3(mk), seq3(mv), bounded)
        out_c = _band_attention(seq3(swq), seq3(swk), seq3(swv), sw_bias, sinks[i], 1,
                                SW_Q_HEADS, False, False, BF16)[0]
        dil_o, dil_l = [], []
        for g, (dq, dk, dv) in enumerate(((dq0, dk0, dv0), (dq1, dk1, dv1), (dq2, dk2, dv2))):
            o, l = _band_attention(seq3(dq), seq3(dk), seq3(dv), dil_bias[g], None,
                                   DIL_PATTERNS[g][1], DIL_HEADS, True, True, F32)
            dil_o.append(flat(o))
            dil_l.append(flat(l))

        router = None
        if i % 2 == 1:
            wr = jnp.pad(w_router[i // 2], ((0, 0), (0, LANES - N_EXPERTS)))
            wr_hi = wr.astype(BF16)
            wr_lo = (wr - wr_hi.astype(F32)).astype(BF16)
            br = jnp.pad(b_router[i // 2], (0, LANES - N_EXPERTS), constant_values=NEG)[None, :]
            router = (jnp.concatenate([wr_hi, wr_lo], axis=1), br)
        merged = _merge(x2, glow, flat(out_a), flat(out_b), flat(out_c), dil_o, dil_l,
                        w_gate_b[i].astype(BF16), b_gate[i][None, :], w_branch[i].astype(BF16),
                        w_out[i].astype(BF16), norm2_g[i][None, :], router)
        if i % 2 == 0:
            x1, h2 = merged
            x2 = _dense_ffn(x1, h2, w_gu_dense[i // 2].astype(BF16), w_down_dense[i // 2].astype(BF16))
        else:
            x1, h2, wts = merged
            x2 = _moe_ffn(x1, h2, wts, w_gu_exp[i // 2].astype(BF16), w_down_exp[i // 2].astype(BF16))
    return x2.reshape(b, s, d)
```
